```python
import jax, jax.numpy as jnp
from jax import lax
import numpy as np

D_MODEL = 1024
BATCH = 16
SEQ = 2048
DEPTH = 2

CHUNK = 64
Q_BLOCK = 128
NORM_EPS = 1e-6
MASK_VALUE = -1e30
TINY = 1e-30

MIX_WIDTH = D_MODEL
HGRN_HEADS = 4
HGRN_WIDTH = MIX_WIDTH // 4
HGRN_KEY_DIM = HGRN_WIDTH // HGRN_HEADS
HGRN_VAL_DIM = HGRN_WIDTH // HGRN_HEADS
POOL_WINDOWS = (2, 4, 8, 16)
POOL_GROUPS = len(POOL_WINDOWS)
POOL_WIDTH = MIX_WIDTH // 4
POOL_GROUP_DIM = POOL_WIDTH // POOL_GROUPS
FOX_HEADS = 8
FOX_WIDTH = MIX_WIDTH // 2
FOX_HEAD_DIM = FOX_WIDTH // FOX_HEADS

IN_WIDTHS = (HGRN_WIDTH, HGRN_WIDTH, HGRN_WIDTH, HGRN_WIDTH,
             POOL_WIDTH, POOL_WIDTH,
             FOX_WIDTH, FOX_WIDTH, FOX_WIDTH, FOX_WIDTH, FOX_HEADS)
IN_WIDTH = int(sum(IN_WIDTHS))
IN_SPLIT_POINTS = tuple(int(v) for v in np.cumsum(IN_WIDTHS)[:-1])

kernel_name = "hybrid_hgrn2_pool_fox_stream_encoder"


def _rmsnorm(x, g):
    xf = x.astype(jnp.float32)
    y = xf * lax.rsqrt(jnp.mean(xf * xf, axis=-1, keepdims=True) + NORM_EPS)
    return (y * g.astype(jnp.float32)).astype(x.dtype)


def _hgrn2_chunkwise(q, k, v, log_f):
    B, T, H, dk = q.shape
    dv = v.shape[-1]
    n = T // CHUNK

    def to_chunks(a):
        return a.astype(jnp.float32).reshape(B, n, CHUNK, H, a.shape[-1]).transpose(1, 0, 3, 2, 4)

    qc, kc, vc, gc = to_chunks(q), to_chunks(k), to_chunks(v), to_chunks(log_f)
    pos = jnp.arange(CHUNK)
    causal = (pos[:, None] >= pos[None, :])[:, :, None]
    causal_f = causal.astype(jnp.float32)

    def step(S, inp):
        qi, ki, vi, gi = inp
        b = jnp.cumsum(gi, axis=2)
        diff = b[:, :, :, None, :] - b[:, :, None, :, :]
        decay = jnp.exp(jnp.where(causal, diff, 0.0)) * causal_f
        attn = jnp.einsum('bhtd,bhsd,bhtsd->bhts', qi, ki, decay)
        o = (jnp.einsum('bhts,bhsv->bhtv', attn, vi)
             + jnp.einsum('bhtd,bhdv->bhtv', qi * jnp.exp(b), S))
        b_last = b[:, :, -1, :]
        S = (jnp.exp(b_last)[..., None] * S
             + jnp.einsum('bhsd,bhsv->bhdv', ki * jnp.exp(b_last[:, :, None, :] - b), vi))
        return S, o

    S0 = jnp.zeros((B, H, dk, dv), jnp.float32)
    _, o = lax.scan(step, S0, (qc, kc, vc, gc))
    return o.transpose(1, 0, 3, 2, 4).reshape(B, T, H, dv)


def _multiscale_pool(u):
    B, T, _ = u.shape
    uf = u.astype(jnp.float32).reshape(B, T, POOL_GROUPS, POOL_GROUP_DIM)
    cs = jnp.pad(jnp.cumsum(uf, axis=1), ((0, 0), (1, 0), (0, 0), (0, 0)))
    t = jnp.arange(T, dtype=jnp.float32)
    outs = []
    for g, w in enumerate(POOL_WINDOWS):
        c = cs[:, :, g]
        upper = c[:, 1:]
        lower = jnp.pad(c[:, :T + 1 - w], ((0, 0), (w - 1, 0), (0, 0)))
        count = jnp.minimum(t + 1.0, float(w))[None, :, None]
        outs.append((upper - lower) / count - uf[:, :, g])
    return jnp.stack(outs, axis=2)


def _forgetting_attention(q, k, v, log_f):
    B, T, H, D = q.shape
    c = jnp.cumsum(log_f, axis=1).transpose(0, 2, 1)
    scale = D ** -0.5
    outs = []
    for i in range(T // Q_BLOCK):
        q0, q1 = i * Q_BLOCK, (i + 1) * Q_BLOCK
        s = jnp.einsum('bqhd,bkhd->bhqk', q[:, q0:q1], k[:, :q1]).astype(jnp.float32) * scale
        mask = (q0 + jnp.arange(Q_BLOCK))[:, None] >= jnp.arange(q1)[None, :]
        bias = jnp.where(mask, c[:, :, q0:q1, None] - c[:, :, None, :q1], 0.0)
        p = jax.nn.softmax(jnp.where(mask, s + bias, MASK_VALUE), axis=-1)
        outs.append(jnp.einsum('bhqk,bkhd->bqhd', p.astype(v.dtype), v[:, :q1]))
    return jnp.concatenate(outs, axis=1)


def setup_inputs(seed: int = 0) -> dict:
    key = jax.random.key(seed)
    ks = jax.random.split(key, 11)
    f32 = jnp.float32
    x = jax.random.normal(ks[0], (BATCH, SEQ, D_MODEL), f32)
    lower_bounds = jax.random.normal(ks[1], (DEPTH, HGRN_WIDTH), f32)
    pre_norm_g = 1.0 + 0.05 * jax.random.normal(ks[2], (DEPTH, D_MODEL), f32)
    w_in = jax.random.normal(ks[3], (DEPTH, D_MODEL, IN_WIDTH), f32) * D_MODEL ** -0.5
    hgrn_norm_g = 1.0 + 0.05 * jax.random.normal(ks[4], (DEPTH, HGRN_WIDTH), f32)
    fox_f_bias = jax.random.uniform(ks[5], (DEPTH, FOX_HEADS), f32, minval=1.0, maxval=4.0)
    pool_w = jax.random.normal(ks[6], (DEPTH, POOL_GROUPS, POOL_GROUP_DIM, POOL_GROUP_DIM), f32) * POOL_GROUP_DIM ** -0.5
    pool_scale = jax.random.uniform(ks[7], (DEPTH, POOL_WIDTH), f32, minval=0.5, maxval=1.5)
    w_out = jax.random.normal(ks[8], (DEPTH, MIX_WIDTH, D_MODEL), f32) * MIX_WIDTH ** -0.5
    post_norm_g = 1.0 + 0.05 * jax.random.normal(ks[9], (DEPTH, D_MODEL), f32)
    return {"x": x, "lower_bounds": lower_bounds, "pre_norm_g": pre_norm_g, "w_in": w_in,
            "hgrn_norm_g": hgrn_norm_g, "fox_f_bias": fox_f_bias, "pool_w": pool_w,
            "pool_scale": pool_scale, "w_out": w_out, "post_norm_g": post_norm_g}


def reference(x, lower_bounds, pre_norm_g, w_in, hgrn_norm_g, fox_f_bias, pool_w, pool_scale, w_out, post_norm_g):
    B, T, _ = x.shape
    p = jax.nn.softmax(lower_bounds.astype(jnp.float32), axis=0)
    lbs = jnp.cumsum(p, axis=0) - p[0]
    for l in range(DEPTH):
        h = _rmsnorm(x, pre_norm_g[l])
        proj = jnp.einsum('btd,de->bte', h, w_in[l])
        q_a, f_a, i_a, g_a, u_b, g_b, q_c, k_c, v_c, g_c, f_c = jnp.split(proj, IN_SPLIT_POINTS, axis=-1)

        lb = lbs[l]
        z = f_a.astype(jnp.float32)
        f_gate = lb + (1.0 - lb) * jax.nn.sigmoid(z)
        log_f_a = jnp.log(jnp.maximum(f_gate, TINY))
        k_a = (1.0 - lb) * jax.nn.sigmoid(-z)
        hshape = (B, T, HGRN_HEADS, HGRN_KEY_DIM)
        o_a = _hgrn2_chunkwise(jax.nn.silu(q_a).reshape(hshape), k_a.reshape(hshape),
                               i_a.reshape(B, T, HGRN_HEADS, HGRN_VAL_DIM), log_f_a.reshape(hshape))
        o_a = _rmsnorm(o_a, hgrn_norm_g[l].reshape(HGRN_HEADS, HGRN_VAL_DIM)).reshape(B, T, HGRN_WIDTH)
        o_a = o_a.astype(x.dtype) * jax.nn.silu(g_a)

        pooled = _multiscale_pool(u_b)
        o_b = jnp.einsum('btgc,gcd->btgd', pooled, pool_w[l].astype(jnp.float32)).reshape(B, T, POOL_WIDTH)
        o_b = (o_b * pool_scale[l].astype(jnp.float32)).astype(x.dtype) * jax.nn.silu(g_b)

        log_f_c = jax.nn.log_sigmoid((f_c + fox_f_bias[l]).astype(jnp.float32))
        fshape = (B, T, FOX_HEADS, FOX_HEAD_DIM)
        o_c = _forgetting_attention(q_c.reshape(fshape), k_c.reshape(fshape), v_c.reshape(fshape), log_f_c)
        o_c = o_c.reshape(B, T, FOX_WIDTH).astype(x.dtype) * jax.nn.silu(g_c)

        mixed = jnp.concatenate([o_a, o_b, o_c], axis=-1)
        y = jnp.einsum('bte,ed->btd', mixed, w_out[l])
        x = x + _rmsnorm(y, post_norm_g[l])
    return x
```

```python
import functools

import jax
import jax.numpy as jnp
from jax import lax
from jax.experimental import pallas as pl
from jax.experimental.pallas import tpu as pltpu

D_MODEL = 1024
DEPTH = 2
NORM_EPS = 1e-6
MASK_VALUE = -1e30
TINY = 1e-30

HGRN_HEADS = 4
HGRN_WIDTH = 256
HEAD_DIM = 64
POOL_WINDOWS = (2, 4, 8, 16)
POOL_WIDTH = 256
FOX_HEADS = 8
FOX_WIDTH = 512
IN_WIDTH = 3592

LANES = 128
IN_WIDTH_PAD = 29 * LANES
VMEM_LIMIT = 56 * 1024 * 1024

COL_HGRN = 0
COL_POOL = 1024
COL_Q = 1536
COL_K = 2048
COL_V = 2560
COL_GC = 3072
COL_FC = 3584

HGRN_CHUNK = 16
PROJ_ROWS = 512
ATT_BLOCK = 256
CUM_BLOCK = 256
POOL_ROWS = 128
POOL_HALO = 16


def _sigmoid_pair(z):
    e = jnp.exp(-jnp.abs(z))
    r = 1.0 / (1.0 + e)
    big, small = r, e * r
    pos = z >= 0
    return jnp.where(pos, big, small), jnp.where(pos, small, big)


def _silu(x):
    s, _ = _sigmoid_pair(x)
    return x * s


def _head_indicator(n, dtype):
    r = lax.broadcasted_iota(jnp.int32, (n, n), 0) // HEAD_DIM
    c = lax.broadcasted_iota(jnp.int32, (n, n), 1) // HEAD_DIM
    return (r == c).astype(dtype)


def _inproj_kernel(x_ref, g_ref, w_ref, o_ref):
    x = x_ref[...]
    ms = jnp.mean(x * x, axis=-1, keepdims=True)
    h = x * lax.rsqrt(ms + NORM_EPS) * g_ref[...]
    o_ref[...] = jnp.dot(h.astype(jnp.bfloat16), w_ref[...], preferred_element_type=jnp.float32)


def _inproj(x2d, g, w):
    m = x2d.shape[0]
    return pl.pallas_call(
        _inproj_kernel,
        grid=(m // PROJ_ROWS,),
        in_specs=[
            pl.BlockSpec((PROJ_ROWS, D_MODEL), lambda i: (i, 0)),
            pl.BlockSpec((1, D_MODEL), lambda i: (0, 0)),
            pl.BlockSpec((D_MODEL, IN_WIDTH_PAD), lambda i: (0, 0)),
        ],
        out_specs=pl.BlockSpec((PROJ_ROWS, IN_WIDTH_PAD), lambda i: (i, 0)),
        out_shape=jax.ShapeDtypeStruct((m, IN_WIDTH_PAD), jnp.float32),
        compiler_params=pltpu.CompilerParams(
            dimension_semantics=("arbitrary",), vmem_limit_bytes=VMEM_LIMIT),
        name="inproj",
    )(x2d, g, w)


def _cumsum_kernel(f_ref, bias_ref, col_ref, row_ref):
    t_len = f_ref.shape[1]
    r = lax.broadcasted_iota(jnp.int32, (CUM_BLOCK, CUM_BLOCK), 0)
    c = lax.broadcasted_iota(jnp.int32, (CUM_BLOCK, CUM_BLOCK), 1)
    tri = (r >= c).astype(jnp.float32)
    carry = jnp.zeros((1, LANES), jnp.float32)
    for blk in range(t_len // CUM_BLOCK):
        rows = pl.ds(blk * CUM_BLOCK, CUM_BLOCK)
        v = f_ref[0, rows, :] + bias_ref[...]
        logf = jnp.minimum(v, 0.0) - jnp.log1p(jnp.exp(-jnp.abs(v)))
        cs = jnp.dot(tri, logf, precision=lax.Precision.HIGHEST,
                     preferred_element_type=jnp.float32) + carry
        col_ref[0, rows, :] = cs
        row_ref[0, :, rows] = cs.T[:FOX_HEADS, :]
        carry = cs[CUM_BLOCK - 1:CUM_BLOCK, :]


def _forget_cumsum(proj, bias_pad):
    b, t, _ = proj.shape
    return pl.pallas_call(
        _cumsum_kernel,
        grid=(b,),
        in_specs=[
            pl.BlockSpec((1, t, LANES), lambda i: (i, 0, COL_FC // LANES)),
            pl.BlockSpec((1, LANES), lambda i: (0, 0)),
        ],
        out_specs=[
            pl.BlockSpec((1, t, LANES), lambda i: (i, 0, 0)),
            pl.BlockSpec((1, FOX_HEADS, t), lambda i: (i, 0, 0)),
        ],
        out_shape=[
            jax.ShapeDtypeStruct((b, t, LANES), jnp.float32),
            jax.ShapeDtypeStruct((b, FOX_HEADS, t), jnp.float32),
        ],
        compiler_params=pltpu.CompilerParams(dimension_semantics=("arbitrary",)),
        name="forget_cumsum",
    )(proj, bias_pad)


def _hgrn_kernel(lb_ref, gn_ref, proj_ref, o_ref, st_ref, *, layer):
    t_len = proj_ref.shape[1]
    w = HGRN_WIDTH
    c_len = HGRN_CHUNK

    raw = lb_ref[...]
    e = jnp.exp(raw - jnp.max(raw, axis=0, keepdims=True))
    p = e / jnp.sum(e, axis=0, keepdims=True)
    lb = jnp.sum(p[0:layer + 1, :], axis=0, keepdims=True) - p[0:1, :]

    ind = _head_indicator(w, jnp.bfloat16)
    ind_f = _head_indicator(w, jnp.float32)
    row = lax.broadcasted_iota(jnp.int32, (c_len, w), 0)
    tri = (lax.broadcasted_iota(jnp.int32, (c_len, c_len), 0)
           >= lax.broadcasted_iota(jnp.int32, (c_len, c_len), 1)).astype(jnp.float32)
    gn = gn_ref[...]

    st_ref[...] = jnp.zeros_like(st_ref)

    def body(c, carry):
        r0 = pl.multiple_of(c * c_len, c_len)
        rows = pl.ds(r0, c_len)
        q = _silu(proj_ref[0, rows, 0:w])
        z = proj_ref[0, rows, w:2 * w]
        v = proj_ref[0, rows, 2 * w:3 * w]
        gate = proj_ref[0, rows, 3 * w:4 * w]

        sig, sig_neg = _sigmoid_pair(z)
        f = lb + (1.0 - lb) * sig
        logf = jnp.log(jnp.maximum(f, TINY))
        k = (1.0 - lb) * sig_neg
        b = jnp.dot(tri, logf, precision=lax.Precision.HIGHEST,
                    preferred_element_type=jnp.float32)

        parts = []
        for s in range(c_len):
            causal = row >= s
            dec = jnp.where(causal, jnp.exp(jnp.where(causal, b - b[s:s + 1, :], 0.0)), 0.0)
            parts.append(q * (k[s:s + 1, :] * dec))
        a = jnp.concatenate(parts, axis=0).astype(jnp.bfloat16)
        a = jnp.dot(a, ind, preferred_element_type=jnp.float32)
        o = jnp.zeros((c_len, w), jnp.float32)
        for s in range(c_len):
            o = o + a[s * c_len:(s + 1) * c_len, :] * v[s:s + 1, :]

        st = st_ref[...]
        qd = (q * jnp.exp(b)).astype(jnp.bfloat16)
        o = o + lax.dot_general(qd, st.astype(jnp.bfloat16), (((1,), (1,)), ((), ())),
                                preferred_element_type=jnp.float32)
        b_last = b[c_len - 1:c_len, :]
        kd = (k * jnp.exp(b_last - b)).astype(jnp.bfloat16)
        upd = lax.dot_general(v.astype(jnp.bfloat16), kd, (((0,), (0,)), ((), ())),
                              preferred_element_type=jnp.float32)
        st_ref[...] = st * jnp.exp(b_last) + upd * ind_f

        ms = jnp.dot((o * o).astype(jnp.bfloat16), ind,
                     preferred_element_type=jnp.float32) * (1.0 / HEAD_DIM)
        y = o * lax.rsqrt(ms + NORM_EPS) * gn
        o_ref[0, rows, :] = y * _silu(gate)
        return carry

    lax.fori_loop(0, t_len // c_len, body, 0)


def _hgrn(proj, lower_bounds, gn, layer):
    b, t, _ = proj.shape
    return pl.pallas_call(
        functools.partial(_hgrn_kernel, layer=layer),
        grid=(b,),
        in_specs=[
            pl.BlockSpec((DEPTH, HGRN_WIDTH), lambda i: (0, 0)),
            pl.BlockSpec((1, HGRN_WIDTH), lambda i: (0, 0)),
            pl.BlockSpec((1, t, 4 * HGRN_WIDTH), lambda i: (i, 0, COL_HGRN // (4 * HGRN_WIDTH))),
        ],
        out_specs=pl.BlockSpec((1, t, HGRN_WIDTH), lambda i: (i, 0, 0)),
        out_shape=jax.ShapeDtypeStruct((b, t, HGRN_WIDTH), jnp.float32),
        scratch_shapes=[pltpu.VMEM((HGRN_WIDTH, HGRN_WIDTH), jnp.float32)],
        compiler_params=pltpu.CompilerParams(
            dimension_semantics=("arbitrary",), vmem_limit_bytes=VMEM_LIMIT),
        name="hgrn2",
    )(lower_bounds, gn, proj)


def _pool_kernel(proj_ref, w_ref, scale_ref, o_ref, pad_ref):
    t_len = proj_ref.shape[1]
    w = POOL_WIDTH
    pad_ref[0:POOL_HALO, :] = jnp.zeros((POOL_HALO, w), jnp.float32)
    pad_ref[POOL_HALO:POOL_HALO + t_len, :] = proj_ref[0, :, 0:w]
    lane = lax.broadcasted_iota(jnp.int32, (POOL_ROWS, w), 1)
    row = lax.broadcasted_iota(jnp.int32, (POOL_ROWS, w), 0)
    group = lane // HEAD_DIM
    wb = w_ref[...]
    scale = scale_ref[...]

    def body(i, carry):
        r0 = pl.multiple_of(i * POOL_ROWS, POOL_ROWS)
        x = pad_ref[pl.ds(r0, POOL_ROWS + POOL_HALO), :]
        s2 = x + pltpu.roll(x, 1, axis=0)
        s4 = s2 + pltpu.roll(s2, 2, axis=0)
        s8 = s4 + pltpu.roll(s4, 4, axis=0)
        s16 = s8 + pltpu.roll(s8, 8, axis=0)
        tpos = (row + r0 + 1).astype(jnp.float32)
        pooled = None
        for gi, (win, sm) in enumerate(zip(POOL_WINDOWS, (s2, s4, s8, s16))):
            val = sm[POOL_HALO:, :] / jnp.minimum(tpos, float(win))
            pooled = val if pooled is None else jnp.where(group == gi, val, pooled)
        u = x[POOL_HALO:, :]
        pooled = pooled - u
        rows = pl.ds(r0, POOL_ROWS)
        mixed = jnp.dot(pooled.astype(jnp.bfloat16), wb, preferred_element_type=jnp.float32)
        o_ref[0, rows, :] = (mixed * scale) * _silu(proj_ref[0, rows, w:2 * w])
        return carry

    lax.fori_loop(0, t_len // POOL_ROWS, body, 0)


def _pool(proj, w_blockdiag, scale):
    b, t, _ = proj.shape
    return pl.pallas_call(
        _pool_kernel,
        grid=(b,),
        in_specs=[
            pl.BlockSpec((1, t, 2 * POOL_WIDTH), lambda i: (i, 0, COL_POOL // (2 * POOL_WIDTH))),
            pl.BlockSpec((POOL_WIDTH, POOL_WIDTH), lambda i: (0, 0)),
            pl.BlockSpec((1, POOL_WIDTH), lambda i: (0, 0)),
        ],
        out_specs=pl.BlockSpec((1, t, POOL_WIDTH), lambda i: (i, 0, 0)),
        out_shape=jax.ShapeDtypeStruct((b, t, POOL_WIDTH), jnp.float32),
        scratch_shapes=[pltpu.VMEM((t + POOL_HALO, POOL_WIDTH), jnp.float32)],
        compiler_params=pltpu.CompilerParams(dimension_semantics=("arbitrary",)),
        name="pool_mixer",
    )(proj, w_blockdiag, scale)


def _fox_kernel(q_ref, k_ref, v_ref, g_ref, ccol_ref, crow_ref, o_ref):
    pair = pl.program_id(1)
    qi = pl.program_id(2)
    bq = ATT_BLOCK
    q = q_ref[0] * (HEAD_DIM ** -0.5)
    lane = lax.broadcasted_iota(jnp.int32, (bq, LANES), 1)
    rpos = lax.broadcasted_iota(jnp.int32, (bq, bq), 0)
    cpos = lax.broadcasted_iota(jnp.int32, (bq, bq), 1)
    lane8 = lax.broadcasted_iota(jnp.int32, (bq, LANES), 1)
    sub8 = lax.broadcasted_iota(jnp.int32, (FOX_HEADS, bq), 0)
    ccol_all = ccol_ref[0]

    outs = []
    for h2 in range(2):
        head = pair * 2 + h2
        in_head = (lane // HEAD_DIM) == h2
        qh = jnp.where(in_head, q, 0.0).astype(jnp.bfloat16)
        c_t = jnp.sum(jnp.where(lane8 == head, ccol_all, 0.0), axis=1, keepdims=True)

        def body(j, carry):
            m, l, acc = carry
            k0 = pl.multiple_of(j * bq, bq)
            kb = k_ref[0, pl.ds(k0, bq), :].astype(jnp.bfloat16)
            vb = v_ref[0, pl.ds(k0, bq), :].astype(jnp.bfloat16)
            s = lax.dot_general(qh, kb, (((1,), (1,)), ((), ())),
                                preferred_element_type=jnp.float32)
            c_s = jnp.sum(jnp.where(sub8 == head, crow_ref[0, :, pl.ds(k0, bq)], 0.0),
                          axis=0, keepdims=True)
            mask = (rpos + qi * bq) >= (cpos + k0)
            s = jnp.where(mask, s + jnp.where(mask, c_t - c_s, 0.0), MASK_VALUE)
            m_new = jnp.maximum(m, jnp.max(s, axis=1, keepdims=True))
            alpha = jnp.exp(m - m_new)
            p = jnp.exp(s - m_new)
            l = alpha * l + jnp.sum(p, axis=1, keepdims=True)
            acc = alpha * acc + jnp.dot(p.astype(jnp.bfloat16), vb,
                                        preferred_element_type=jnp.float32)
            return m_new, l, acc

        init = (jnp.full((bq, 1), MASK_VALUE, jnp.float32),
                jnp.zeros((bq, 1), jnp.float32),
                jnp.zeros((bq, LANES), jnp.float32))
        m, l, acc = lax.fori_loop(0, qi + 1, body, init)
        outs.append(acc / l)
    o = jnp.where((lane // HEAD_DIM) == 0, outs[0], outs[1])
    o_ref[0] = o * _silu(g_ref[0])


def _fox(proj, ccol, crow):
    b, t, _ = proj.shape
    pairs = FOX_HEADS // 2
    blk = lambda col: col // LANES
    return pl.pallas_call(
        _fox_kernel,
        grid=(b, pairs, t // ATT_BLOCK),
        in_specs=[
            pl.BlockSpec((1, ATT_BLOCK, LANES), lambda i, p, q: (i, q, blk(COL_Q) + p)),
            pl.BlockSpec((1, t, LANES), lambda i, p, q: (i, 0, blk(COL_K) + p)),
            pl.BlockSpec((1, t, LANES), lambda i, p, q: (i, 0, blk(COL_V) + p)),
            pl.BlockSpec((1, ATT_BLOCK, LANES), lambda i, p, q: (i, q, blk(COL_GC) + p)),
            pl.BlockSpec((1, ATT_BLOCK, LANES), lambda i, p, q: (i, q, 0)),
            pl.BlockSpec((1, FOX_HEADS, t), lambda i, p, q: (i, 0, 0)),
        ],
        out_specs=pl.BlockSpec((1, ATT_BLOCK, LANES), lambda i, p, q: (i, q, p)),
        out_shape=jax.ShapeDtypeStruct((b, t, FOX_WIDTH), jnp.float32),
        compiler_params=pltpu.CompilerParams(
            dimension_semantics=("arbitrary", "arbitrary", "arbitrary")),
        name="fox_attention",
    )(proj, proj, proj, proj, ccol, crow)


def _outproj_kernel(x_ref, a_ref, b_ref, c_ref, w_ref, g_ref, o_ref):
    mixed = jnp.concatenate([a_ref[...], b_ref[...], c_ref[...]], axis=-1).astype(jnp.bfloat16)
    y = jnp.dot(mixed, w_ref[...], preferred_element_type=jnp.float32)
    ms = jnp.mean(y * y, axis=-1, keepdims=True)
    o_ref[...] = x_ref[...] + y * lax.rsqrt(ms + NORM_EPS) * g_ref[...]


def _outproj(x2d, o_a, o_b, o_c, w, g):
    m = x2d.shape[0]
    row_spec = lambda width: pl.BlockSpec((PROJ_ROWS, width), lambda i: (i, 0))
    return pl.pallas_call(
        _outproj_kernel,
        grid=(m // PROJ_ROWS,),
        in_specs=[
            row_spec(D_MODEL), row_spec(HGRN_WIDTH), row_spec(POOL_WIDTH), row_spec(FOX_WIDTH),
            pl.BlockSpec((D_MODEL, D_MODEL), lambda i: (0, 0)),
            pl.BlockSpec((1, D_MODEL), lambda i: (0, 0)),
        ],
        out_specs=row_spec(D_MODEL),
        out_shape=jax.ShapeDtypeStruct((m, D_MODEL), jnp.float32),
        compiler_params=pltpu.CompilerParams(
            dimension_semantics=("arbitrary",), vmem_limit_bytes=VMEM_LIMIT),
        name="outproj",
    )(x2d, o_a, o_b, o_c, w, g)


def kernel(x, lower_bounds, pre_norm_g, w_in, hgrn_norm_g, fox_f_bias, pool_w, pool_scale, w_out,
           post_norm_g):
    b, t, d = x.shape
    m = b * t
    x2d = x.reshape(m, d)
    for layer in range(DEPTH):
        w_in_l = jnp.pad(w_in[layer], ((0, 0), (0, IN_WIDTH_PAD - IN_WIDTH))).astype(jnp.bfloat16)
        bias_pad = jnp.pad(fox_f_bias[layer], (0, LANES - FOX_HEADS)).reshape(1, LANES)
        pool_bd = jax.scipy.linalg.block_diag(*pool_w[layer]).astype(jnp.bfloat16)

        proj = _inproj(x2d, pre_norm_g[layer].reshape(1, d), w_in_l).reshape(b, t, IN_WIDTH_PAD)
        ccol, crow = _forget_cumsum(proj, bias_pad)
        o_a = _hgrn(proj, lower_bounds, hgrn_norm_g[layer].reshape(1, HGRN_WIDTH), layer)
        o_b = _pool(proj, pool_bd, pool_scale[layer].reshape(1, POOL_WIDTH))
        o_c = _fox(proj, ccol, crow)
        x2d = _outproj(x2d, o_a.reshape(m, HGRN_WIDTH), o_b.reshape(m, POOL_WIDTH),
                       o_c.reshape(m, FOX_WIDTH), w_out[layer].astype(jnp.bfloat16),
                       post_norm_g[layer].reshape(1, d))
    return x2d.reshape(b, t, d)
```

```python
import functools
import math

import jax
import jax.numpy as jnp
from jax import lax
from jax.experimental import pallas as pl
from jax.experimental.pallas import tpu as pltpu

D_MODEL = 1024
DEPTH = 2
NORM_EPS = 1e-6
MASK_VALUE = -1e30
TINY = 1e-30

HGRN_HEADS = 4
HGRN_WIDTH = 256
HEAD_DIM = 64
POOL_WINDOWS = (2, 4, 8, 16)
POOL_WIDTH = 256
FOX_HEADS = 8
FOX_WIDTH = 512
IN_WIDTH = 3592

LANES = 128
IN_WIDTH_PAD = 29 * LANES
VMEM_LIMIT = 56 * 1024 * 1024

COL_HGRN = 0
COL_POOL = 1024
COL_Q = 1536
COL_FC = 3584

HGRN_CHUNK = 16
PROJ_ROWS = 512
ATT_BLOCK = 512
ATT_QUERY_TILE = 256
CUM_BLOCK = 256
POOL_ROWS = 128
POOL_HALO = 16

LOG2E = math.log2(math.e)
BIAS_SLOT = 8


def _sigmoid_pair(z):
    e = jnp.exp(-jnp.abs(z))
    r = 1.0 / (1.0 + e)
    big, small = r, e * r
    pos = z >= 0
    return jnp.where(pos, big, small), jnp.where(pos, small, big)


def _silu(x):
    s, _ = _sigmoid_pair(x)
    return x * s


def _head_indicator(n, dtype):
    r = lax.broadcasted_iota(jnp.int32, (n, n), 0) // HEAD_DIM
    c = lax.broadcasted_iota(jnp.int32, (n, n), 1) // HEAD_DIM
    return (r == c).astype(dtype)


def _inproj_kernel(x_ref, g_ref, w_ref, pa_ref, pc_ref, pf_ref):
    x = x_ref[...]
    ms = jnp.mean(x * x, axis=-1, keepdims=True)
    h = (x * lax.rsqrt(ms + NORM_EPS) * g_ref[...]).astype(jnp.bfloat16)
    pa_ref[...] = jnp.dot(h, w_ref[:, 0:COL_Q], preferred_element_type=jnp.float32)
    pc_ref[...] = jnp.dot(h, w_ref[:, COL_Q:COL_FC],
                          preferred_element_type=jnp.float32).astype(jnp.bfloat16)
    pf_ref[...] = jnp.dot(h, w_ref[:, COL_FC:IN_WIDTH_PAD], preferred_element_type=jnp.float32)


def _inproj(x2d, g, w):
    m = x2d.shape[0]
    widths = (COL_Q, COL_FC - COL_Q, IN_WIDTH_PAD - COL_FC)
    dtypes = (jnp.float32, jnp.bfloat16, jnp.float32)
    return pl.pallas_call(
        _inproj_kernel,
        grid=(m // PROJ_ROWS,),
        in_specs=[
            pl.BlockSpec((PROJ_ROWS, D_MODEL), lambda i: (i, 0)),
            pl.BlockSpec((1, D_MODEL), lambda i: (0, 0)),
            pl.BlockSpec((D_MODEL, IN_WIDTH_PAD), lambda i: (0, 0)),
        ],
        out_specs=[pl.BlockSpec((PROJ_ROWS, wd), lambda i: (i, 0)) for wd in widths],
        out_shape=[jax.ShapeDtypeStruct((m, wd), dt) for wd, dt in zip(widths, dtypes)],
        compiler_params=pltpu.CompilerParams(
            dimension_semantics=("arbitrary",), vmem_limit_bytes=VMEM_LIMIT),
        name="inproj",
    )(x2d, g, w)


def _bias_placement(part, negate_offset):
    r = lax.broadcasted_iota(jnp.int32, (LANES, LANES), 0)
    c = lax.broadcasted_iota(jnp.int32, (LANES, LANES), 1)
    return ((c == r * BIAS_SLOT + part + negate_offset) & (r < FOX_HEADS)).astype(jnp.bfloat16)


def _cumsum_kernel(f_ref, bias_ref, qx_ref, kx_ref):
    t_len = f_ref.shape[1]
    r = lax.broadcasted_iota(jnp.int32, (CUM_BLOCK, CUM_BLOCK), 0)
    c = lax.broadcasted_iota(jnp.int32, (CUM_BLOCK, CUM_BLOCK), 1)
    tri = (r >= c).astype(jnp.float32)
    lane = lax.broadcasted_iota(jnp.int32, (CUM_BLOCK, LANES), 1)
    slot = lane % BIAS_SLOT
    used = lane < FOX_HEADS * BIAS_SLOT
    ones_q = (used & (slot >= 3) & (slot < 6)).astype(jnp.float32)
    ones_k = (used & (slot < 3)).astype(jnp.float32)
    place_q = [_bias_placement(part, 0) for part in range(3)]
    place_k = [_bias_placement(part, 3) for part in range(3)]
    carry = jnp.zeros((1, LANES), jnp.float32)
    for blk in range(t_len // CUM_BLOCK):
        rows = pl.ds(blk * CUM_BLOCK, CUM_BLOCK)
        v = f_ref[0, rows, :] + bias_ref[...]
        logf = jnp.minimum(v, 0.0) - jnp.log1p(jnp.exp(-jnp.abs(v)))
        cs = jnp.dot(tri, logf, precision=lax.Precision.HIGHEST,
                     preferred_element_type=jnp.float32) + carry
        carry = cs[CUM_BLOCK - 1:CUM_BLOCK, :]
        rest = cs * LOG2E
        spread_q = ones_q
        spread_k = ones_k
        for part in range(3):
            piece = rest.astype(jnp.bfloat16)
            rest = rest - piece.astype(jnp.float32)
            spread_q = spread_q + jnp.dot(piece, place_q[part], preferred_element_type=jnp.float32)
            spread_k = spread_k - jnp.dot(piece, place_k[part], preferred_element_type=jnp.float32)
        qx_ref[0, rows, :] = spread_q.astype(jnp.bfloat16)
        kx_ref[0, rows, :] = spread_k.astype(jnp.bfloat16)


def _forget_cumsum(pf, bias_pad):
    b, t, _ = pf.shape
    spec = pl.BlockSpec((1, t, LANES), lambda i: (i, 0, 0))
    return pl.pallas_call(
        _cumsum_kernel,
        grid=(b,),
        in_specs=[spec, pl.BlockSpec((1, LANES), lambda i: (0, 0))],
        out_specs=[spec, spec],
        out_shape=[jax.ShapeDtypeStruct((b, t, LANES), jnp.bfloat16)] * 2,
        compiler_params=pltpu.CompilerParams(dimension_semantics=("arbitrary",)),
        name="forget_cumsum",
    )(pf, bias_pad)


def _hgrn_kernel(lb_ref, gn_ref, proj_ref, o_ref, st_ref, *, layer):
    t_len = proj_ref.shape[1]
    w = HGRN_WIDTH
    c_len = HGRN_CHUNK

    raw = lb_ref[...]
    e = jnp.exp(raw - jnp.max(raw, axis=0, keepdims=True))
    p = e / jnp.sum(e, axis=0, keepdims=True)
    lb = jnp.sum(p[0:layer + 1, :], axis=0, keepdims=True) - p[0:1, :]

    ind = _head_indicator(w, jnp.bfloat16)
    ind_f = _head_indicator(w, jnp.float32)
    row = lax.broadcasted_iota(jnp.int32, (c_len, w), 0)
    tri = (lax.broadcasted_iota(jnp.int32, (c_len, c_len), 0)
           >= lax.broadcasted_iota(jnp.int32, (c_len, c_len), 1)).astype(jnp.float32)
    gn = gn_ref[...]

    st_ref[...] = jnp.zeros_like(st_ref)

    def body(c, carry):
        r0 = pl.multiple_of(c * c_len, c_len)
        rows = pl.ds(r0, c_len)
        q = _silu(proj_ref[0, rows, 0:w])
        z = proj_ref[0, rows, w:2 * w]
        v = proj_ref[0, rows, 2 * w:3 * w]
        gate = proj_ref[0, rows, 3 * w:4 * w]

        sig, sig_neg = _sigmoid_pair(z)
        f = lb + (1.0 - lb) * sig
        logf = jnp.log(jnp.maximum(f, TINY))
        k = (1.0 - lb) * sig_neg
        b = jnp.dot(tri, logf, precision=lax.Precision.HIGHEST,
                    preferred_element_type=jnp.float32)

        parts = []
        for s in range(c_len):
            causal = row >= s
            dec = jnp.where(causal, jnp.exp(jnp.where(causal, b - b[s:s + 1, :], 0.0)), 0.0)
            parts.append(q * (k[s:s + 1, :] * dec))
        a = jnp.concatenate(parts, axis=0).astype(jnp.bfloat16)
        a = jnp.dot(a, ind, preferred_element_type=jnp.float32)
        o = jnp.zeros((c_len, w), jnp.float32)
        for s in range(c_len):
            o = o + a[s * c_len:(s + 1) * c_len, :] * v[s:s + 1, :]

        st = st_ref[...]
        qd = (q * jnp.exp(b)).astype(jnp.bfloat16)
        o = o + lax.dot_general(qd, st.astype(jnp.bfloat16), (((1,), (1,)), ((), ())),
                                preferred_element_type=jnp.float32)
        b_last = b[c_len - 1:c_len, :]
        kd = (k * jnp.exp(b_last - b)).astype(jnp.bfloat16)
        upd = lax.dot_general(v.astype(jnp.bfloat16), kd, (((0,), (0,)), ((), ())),
                              preferred_element_type=jnp.float32)
        st_ref[...] = st * jnp.exp(b_last) + upd * ind_f

        ms = jnp.dot((o * o).astype(jnp.bfloat16), ind,
                     preferred_element_type=jnp.float32) * (1.0 / HEAD_DIM)
        y = o * lax.rsqrt(ms + NORM_EPS) * gn
        o_ref[0, rows, :] = y * _silu(gate)
        return carry

    lax.fori_loop(0, t_len // c_len, body, 0)


def _hgrn(pa, lower_bounds, gn, layer):
    b, t, _ = pa.shape
    return pl.pallas_call(
        functools.partial(_hgrn_kernel, layer=layer),
        grid=(b,),
        in_specs=[
            pl.BlockSpec((DEPTH, HGRN_WIDTH), lambda i: (0, 0)),
            pl.BlockSpec((1, HGRN_WIDTH), lambda i: (0, 0)),
            pl.BlockSpec((1, t, 4 * HGRN_WIDTH), lambda i: (i, 0, COL_HGRN // (4 * HGRN_WIDTH))),
        ],
        out_specs=pl.BlockSpec((1, t, HGRN_WIDTH), lambda i: (i, 0, 0)),
        out_shape=jax.ShapeDtypeStruct((b, t, HGRN_WIDTH), jnp.float32),
        scratch_shapes=[pltpu.VMEM((HGRN_WIDTH, HGRN_WIDTH), jnp.float32)],
        compiler_params=pltpu.CompilerParams(
            dimension_semantics=("arbitrary",), vmem_limit_bytes=VMEM_LIMIT),
        name="hgrn2",
    )(lower_bounds, gn, pa)


def _pool_kernel(proj_ref, w_ref, scale_ref, o_ref, pad_ref):
    t_len = proj_ref.shape[1]
    w = POOL_WIDTH
    pad_ref[0:POOL_HALO, :] = jnp.zeros((POOL_HALO, w), jnp.float32)
    pad_ref[POOL_HALO:POOL_HALO + t_len, :] = proj_ref[0, :, 0:w]
    lane = lax.broadcasted_iota(jnp.int32, (POOL_ROWS, w), 1)
    row = lax.broadcasted_iota(jnp.int32, (POOL_ROWS, w), 0)
    group = lane // HEAD_DIM
    wb = w_ref[...]
    scale = scale_ref[...]

    def body(i, carry):
        r0 = pl.multiple_of(i * POOL_ROWS, POOL_ROWS)
        x = pad_ref[pl.ds(r0, POOL_ROWS + POOL_HALO), :]
        s2 = x + pltpu.roll(x, 1, axis=0)
        s4 = s2 + pltpu.roll(s2, 2, axis=0)
        s8 = s4 + pltpu.roll(s4, 4, axis=0)
        s16 = s8 + pltpu.roll(s8, 8, axis=0)
        tpos = (row + r0 + 1).astype(jnp.float32)
        pooled = None
        for gi, (win, sm) in enumerate(zip(POOL_WINDOWS, (s2, s4, s8, s16))):
            val = sm[POOL_HALO:, :] / jnp.minimum(tpos, float(win))
            pooled = val if pooled is None else jnp.where(group == gi, val, pooled)
        u = x[POOL_HALO:, :]
        pooled = pooled - u
        rows = pl.ds(r0, POOL_ROWS)
        mixed = jnp.dot(pooled.astype(jnp.bfloat16), wb, preferred_element_type=jnp.float32)
        o_ref[0, rows, :] = (mixed * scale) * _silu(proj_ref[0, rows, w:2 * w])
        return carry

    lax.fori_loop(0, t_len // POOL_ROWS, body, 0)


def _pool(pa, w_blockdiag, scale):
    b, t, _ = pa.shape
    return pl.pallas_call(
        _pool_kernel,
        grid=(b,),
        in_specs=[
            pl.BlockSpec((1, t, 2 * POOL_WIDTH), lambda i: (i, 0, COL_POOL // (2 * POOL_WIDTH))),
            pl.BlockSpec((POOL_WIDTH, POOL_WIDTH), lambda i: (0, 0)),
            pl.BlockSpec((1, POOL_WIDTH), lambda i: (0, 0)),
        ],
        out_specs=pl.BlockSpec((1, t, POOL_WIDTH), lambda i: (i, 0, 0)),
        out_shape=jax.ShapeDtypeStruct((b, t, POOL_WIDTH), jnp.float32),
        scratch_shapes=[pltpu.VMEM((t + POOL_HALO, POOL_WIDTH), jnp.float32)],
        compiler_params=pltpu.CompilerParams(dimension_semantics=("arbitrary",)),
        name="pool_mixer",
    )(pa, w_blockdiag, scale)


def _fox_kernel(q_ref, k_ref, v_ref, g_ref, qx_ref, kx_ref, o_ref, kcat_ref, vaug_ref, acc_ref,
                st_ref):
    pair = pl.program_id(1)
    qi = pl.program_id(2)
    bq = ATT_BLOCK
    lane = lax.broadcasted_iota(jnp.int32, (bq, LANES), 1)

    @pl.when(qi == 0)
    def _():
        t_len = k_ref.shape[1]
        kcat_ref[:, 0:LANES] = k_ref[0]
        kcat_ref[:, LANES:2 * LANES] = kx_ref[0]
        vt = v_ref[0].astype(jnp.float32).T
        row_t = lax.broadcasted_iota(jnp.int32, (LANES, t_len), 0)
        vaug_ref[0] = jnp.where(row_t < HEAD_DIM, vt, 1.0).astype(jnp.bfloat16)
        vaug_ref[1] = jnp.where(row_t >= HEAD_DIM, vt, 1.0).astype(jnp.bfloat16)

    qf = q_ref[0].astype(jnp.float32) * (HEAD_DIM ** -0.5 * LOG2E)
    qx = qx_ref[0]
    qcat = []
    for h2 in range(2):
        qh = jnp.where((lane // HEAD_DIM) == h2, qf, 0.0).astype(jnp.bfloat16)
        qxh = jnp.where((lane // BIAS_SLOT) == pair * 2 + h2, qx, jnp.zeros_like(qx))
        qcat.append(jnp.concatenate([qh, qxh], axis=1))
    acc_ref[...] = jnp.zeros_like(acc_ref)

    qt = ATT_QUERY_TILE
    units = [(h2, c) for h2 in range(2) for c in range(bq // qt)]
    qtile = {(h2, c): qcat[h2][c * qt:(c + 1) * qt, :] for h2, c in units}

    def scores(slot, k0, unit, nk):
        st_ref[slot, 0:nk, :] = lax.dot_general(
            kcat_ref[pl.ds(k0, nk), :], qtile[unit], (((1,), (1,)), ((), ())),
            preferred_element_type=jnp.float32)

    def consume(slot, k0, unit, nk, m_prev, diagonal):
        h2, c = unit
        st = st_ref[slot, 0:nk, :]
        if diagonal:
            key = lax.broadcasted_iota(jnp.int32, (nk, qt), 0)
            query = lax.broadcasted_iota(jnp.int32, (nk, qt), 1) + c * qt
            st = jnp.where(query >= key, st, MASK_VALUE)
        m_new = jnp.maximum(m_prev, jnp.max(st, axis=0, keepdims=True))
        pt = jnp.exp2(st - m_new).astype(jnp.bfloat16)
        alpha = jnp.exp2(m_prev - m_new)
        cols = slice(c * qt, (c + 1) * qt)
        acc_ref[h2, :, cols] = acc_ref[h2, :, cols] * alpha + jnp.dot(
            vaug_ref[h2, :, pl.ds(k0, nk)], pt, preferred_element_type=jnp.float32)
        return m_new

    def full_block(j, ms):
        k0 = pl.multiple_of(j * bq, bq)
        ms = list(ms)
        for u, unit in enumerate(units):
            if u + 1 < len(units):
                scores((u + 1) % 2, k0, units[u + 1], bq)
            else:
                scores((u + 1) % 2, k0 + bq, units[0], bq)
            ms[u] = consume(u % 2, k0, unit, bq, ms[u], False)
        return tuple(ms)

    def diag_keys(unit):
        return (unit[1] + 1) * qt

    scores(0, 0, units[0], bq)
    m_init = (jnp.full((1, qt), MASK_VALUE, jnp.float32),) * len(units)
    ms = list(lax.fori_loop(0, qi, full_block, m_init))
    k0 = pl.multiple_of(qi * bq, bq)
    for u, unit in enumerate(units):
        if u + 1 < len(units):
            scores((u + 1) % 2, k0, units[u + 1], diag_keys(units[u + 1]))
        ms[u] = consume(u % 2, k0, unit, diag_keys(unit), ms[u], True)

    a0 = acc_ref[0]
    a1 = acc_ref[1]
    ot = jnp.concatenate([a0[0:HEAD_DIM, :] / a0[HEAD_DIM:HEAD_DIM + 1, :],
                          a1[HEAD_DIM:, :] / a1[0:1, :]], axis=0)
    o_ref[0] = ot.T * _silu(g_ref[0].astype(jnp.float32))


def _fox(pc, qx, kx):
    b, t, _ = pc.shape
    pairs = FOX_HEADS // 2
    blocks_per_section = FOX_WIDTH // LANES
    q_spec = lambda sec: pl.BlockSpec(
        (1, ATT_BLOCK, LANES), lambda i, p, q: (i, q, sec * blocks_per_section + p))
    kv_spec = lambda sec: pl.BlockSpec(
        (1, t, LANES), lambda i, p, q: (i, 0, sec * blocks_per_section + p))
    return pl.pallas_call(
        _fox_kernel,
        grid=(b, pairs, t // ATT_BLOCK),
        in_specs=[
            q_spec(0), kv_spec(1), kv_spec(2), q_spec(3),
            pl.BlockSpec((1, ATT_BLOCK, LANES), lambda i, p, q: (i, q, 0)),
            pl.BlockSpec((1, t, LANES), lambda i, p, q: (i, 0, 0)),
        ],
        out_specs=pl.BlockSpec((1, ATT_BLOCK, LANES), lambda i, p, q: (i, q, p)),
        out_shape=jax.ShapeDtypeStruct((b, t, FOX_WIDTH), jnp.float32),
        scratch_shapes=[
            pltpu.VMEM((t, 2 * LANES), jnp.bfloat16),
            pltpu.VMEM((2, LANES, t), jnp.bfloat16),
            pltpu.VMEM((2, LANES, ATT_BLOCK), jnp.float32),
            pltpu.VMEM((2, ATT_BLOCK, ATT_QUERY_TILE), jnp.float32),
        ],
        compiler_params=pltpu.CompilerParams(
            dimension_semantics=("arbitrary", "arbitrary", "arbitrary"),
            vmem_limit_bytes=VMEM_LIMIT),
        name="fox_attention",
    )(pc, pc, pc, pc, qx, kx)


def _outproj_kernel(x_ref, a_ref, b_ref, c_ref, w_ref, g_ref, o_ref):
    mixed = jnp.concatenate([a_ref[...], b_ref[...], c_ref[...]], axis=-1).astype(jnp.bfloat16)
    y = jnp.dot(mixed, w_ref[...], preferred_element_type=jnp.float32)
    ms = jnp.mean(y * y, axis=-1, keepdims=True)
    o_ref[...] = x_ref[...] + y * lax.rsqrt(ms + NORM_EPS) * g_ref[...]


def _outproj(x2d, o_a, o_b, o_c, w, g):
    m = x2d.shape[0]
    row_spec = lambda width: pl.BlockSpec((PROJ_ROWS, width), lambda i: (i, 0))
    return pl.pallas_call(
        _outproj_kernel,
        grid=(m // PROJ_ROWS,),
        in_specs=[
            row_spec(D_MODEL), row_spec(HGRN_WIDTH), row_spec(POOL_WIDTH), row_spec(FOX_WIDTH),
            pl.BlockSpec((D_MODEL, D_MODEL), lambda i: (0, 0)),
            pl.BlockSpec((1, D_MODEL), lambda i: (0, 0)),
        ],
        out_specs=row_spec(D_MODEL),
        out_shape=jax.ShapeDtypeStruct((m, D_MODEL), jnp.float32),
        compiler_params=pltpu.CompilerParams(
            dimension_semantics=("arbitrary",), vmem_limit_bytes=VMEM_LIMIT),
        name="outproj",
    )(x2d, o_a, o_b, o_c, w, g)


def kernel(x, lower_bounds, pre_norm_g, w_in, hgrn_norm_g, fox_f_bias, pool_w, pool_scale, w_out,
           post_norm_g):
    b, t, d = x.shape
    m = b * t
    x2d = x.reshape(m, d)
    for layer in range(DEPTH):
        w_in_l = jnp.pad(w_in[layer], ((0, 0), (0, IN_WIDTH_PAD - IN_WIDTH))).astype(jnp.bfloat16)
        bias_pad = jnp.pad(fox_f_bias[layer], (0, LANES - FOX_HEADS)).reshape(1, LANES)
        pool_bd = jax.scipy.linalg.block_diag(*pool_w[layer]).astype(jnp.bfloat16)

        pa, pc, pf = _inproj(x2d, pre_norm_g[layer].reshape(1, d), w_in_l)
        pa = pa.reshape(b, t, COL_Q)
        pc = pc.reshape(b, t, COL_FC - COL_Q)
        qx, kx = _forget_cumsum(pf.reshape(b, t, LANES), bias_pad)
        o_a = _hgrn(pa, lower_bounds, hgrn_norm_g[layer].reshape(1, HGRN_WIDTH), layer)
        o_b = _pool(pa, pool_bd, pool_scale[layer].reshape(1, POOL_WIDTH))
        o_c = _fox(pc, qx, kx)
        x2d = _outproj(x2d, o_a.reshape(m, HGRN_WIDTH), o_b.reshape(m, POOL_WIDTH),
                       o_c.reshape(m, FOX_WIDTH), w_out[layer].astype(jnp.bfloat16),
                       post_norm_g[layer].reshape(1, d))
    return x2d.reshape(b, t, d)
```

```python
import functools
import math

import jax
import jax.numpy as jnp
from jax import lax
from jax.experimental import pallas as pl
from jax.experimental.pallas import tpu as pltpu

D_MODEL = 1024
DEPTH = 2
NORM_EPS = 1e-6
MASK_VALUE = -1e30
TINY = 1e-30

HGRN_HEADS = 4
HGRN_WIDTH = 256
HEAD_DIM = 64
POOL_WINDOWS = (2, 4, 8, 16)
POOL_WIDTH = 256
FOX_HEADS = 8
FOX_WIDTH = 512
IN_WIDTH = 3592

LANES = 128
SUBLANES = 8
IN_WIDTH_PAD = 29 * LANES
VMEM_LIMIT = 56 * 1024 * 1024

COL_HGRN = 0
COL_POOL = 1024
COL_Q = 1536
COL_FC = 3584

HGRN_CHUNK = 16
HGRN_BLOCK = 64
PROJ_ROWS = 512
ATT_BLOCK = 512
ATT_QUERY_TILE = 256
CUM_BLOCK = 256
POOL_ROWS = 128
POOL_HALO = 16

LOG2E = math.log2(math.e)
BIAS_SLOT = 8


def _sigmoid_pair(z):
    e = jnp.exp(-jnp.abs(z))
    r = 1.0 / (1.0 + e)
    big, small = r, e * r
    pos = z >= 0
    return jnp.where(pos, big, small), jnp.where(pos, small, big)


def _silu(x):
    s, _ = _sigmoid_pair(x)
    return x * s


def _head_indicator(n, dtype):
    r = lax.broadcasted_iota(jnp.int32, (n, n), 0) // HEAD_DIM
    c = lax.broadcasted_iota(jnp.int32, (n, n), 1) // HEAD_DIM
    return (r == c).astype(dtype)


def _inproj_kernel(x_ref, g_ref, w_ref, pa_ref, pc_ref, pf_ref):
    x = x_ref[...]
    ms = jnp.mean(x * x, axis=-1, keepdims=True)
    h = (x * lax.rsqrt(ms + NORM_EPS) * g_ref[...]).astype(jnp.bfloat16)
    pa_ref[...] = jnp.dot(h, w_ref[:, 0:COL_Q], preferred_element_type=jnp.float32)
    pc_ref[...] = jnp.dot(h, w_ref[:, COL_Q:COL_FC],
                          preferred_element_type=jnp.float32).astype(jnp.bfloat16)
    pf_ref[...] = jnp.dot(h, w_ref[:, COL_FC:IN_WIDTH_PAD], preferred_element_type=jnp.float32)


def _inproj(x2d, g, w):
    m = x2d.shape[0]
    widths = (COL_Q, COL_FC - COL_Q, IN_WIDTH_PAD - COL_FC)
    dtypes = (jnp.float32, jnp.bfloat16, jnp.float32)
    return pl.pallas_call(
        _inproj_kernel,
        grid=(m // PROJ_ROWS,),
        in_specs=[
            pl.BlockSpec((PROJ_ROWS, D_MODEL), lambda i: (i, 0)),
            pl.BlockSpec((1, D_MODEL), lambda i: (0, 0)),
            pl.BlockSpec((D_MODEL, IN_WIDTH_PAD), lambda i: (0, 0)),
        ],
        out_specs=[pl.BlockSpec((PROJ_ROWS, wd), lambda i: (i, 0)) for wd in widths],
        out_shape=[jax.ShapeDtypeStruct((m, wd), dt) for wd, dt in zip(widths, dtypes)],
        compiler_params=pltpu.CompilerParams(
            dimension_semantics=("arbitrary",), vmem_limit_bytes=VMEM_LIMIT),
        name="inproj",
    )(x2d, g, w)


def _bias_placement(part, negate_offset):
    r = lax.broadcasted_iota(jnp.int32, (LANES, LANES), 0)
    c = lax.broadcasted_iota(jnp.int32, (LANES, LANES), 1)
    return ((c == r * BIAS_SLOT + part + negate_offset) & (r < FOX_HEADS)).astype(jnp.bfloat16)


def _cumsum_kernel(f_ref, bias_ref, qx_ref, kx_ref):
    t_len = f_ref.shape[1]
    r = lax.broadcasted_iota(jnp.int32, (CUM_BLOCK, CUM_BLOCK), 0)
    c = lax.broadcasted_iota(jnp.int32, (CUM_BLOCK, CUM_BLOCK), 1)
    tri = (r >= c).astype(jnp.float32)
    lane = lax.broadcasted_iota(jnp.int32, (CUM_BLOCK, LANES), 1)
    slot = lane % BIAS_SLOT
    used = lane < FOX_HEADS * BIAS_SLOT
    ones_q = (used & (slot >= 3) & (slot < 6)).astype(jnp.float32)
    ones_k = (used & (slot < 3)).astype(jnp.float32)
    place_q = [_bias_placement(part, 0) for part in range(3)]
    place_k = [_bias_placement(part, 3) for part in range(3)]
    carry = jnp.zeros((1, LANES), jnp.float32)
    for blk in range(t_len // CUM_BLOCK):
        rows = pl.ds(blk * CUM_BLOCK, CUM_BLOCK)
        v = f_ref[0, rows, :] + bias_ref[...]
        logf = jnp.minimum(v, 0.0) - jnp.log1p(jnp.exp(-jnp.abs(v)))
        cs = jnp.dot(tri, logf, precision=lax.Precision.HIGHEST,
                     preferred_element_type=jnp.float32) + carry
        carry = cs[CUM_BLOCK - 1:CUM_BLOCK, :]
        rest = cs * LOG2E
        spread_q = ones_q
        spread_k = ones_k
        for part in range(3):
            piece = rest.astype(jnp.bfloat16)
            rest = rest - piece.astype(jnp.float32)
            spread_q = spread_q + jnp.dot(piece, place_q[part], preferred_element_type=jnp.float32)
            spread_k = spread_k - jnp.dot(piece, place_k[part], preferred_element_type=jnp.float32)
        qx_ref[0, rows, :] = spread_q.astype(jnp.bfloat16)
        kx_ref[0, rows, :] = spread_k.astype(jnp.bfloat16)


def _forget_cumsum(pf, bias_pad):
    b, t, _ = pf.shape
    spec = pl.BlockSpec((1, t, LANES), lambda i: (i, 0, 0))
    return pl.pallas_call(
        _cumsum_kernel,
        grid=(b,),
        in_specs=[spec, pl.BlockSpec((1, LANES), lambda i: (0, 0))],
        out_specs=[spec, spec],
        out_shape=[jax.ShapeDtypeStruct((b, t, LANES), jnp.bfloat16)] * 2,
        compiler_params=pltpu.CompilerParams(dimension_semantics=("arbitrary",)),
        name="forget_cumsum",
    )(pf, bias_pad)


def _hgrn_kernel(lb_ref, gn_ref, proj_ref, o_ref, st_ref, *slots, layer):
    t_len = proj_ref.shape[1]
    w = HGRN_WIDTH
    c_len = HGRN_CHUNK
    half = c_len // 2
    rb = HGRN_BLOCK
    n_chunks = rb // c_len
    pairs = w // LANES

    raw = lb_ref[...]
    e = jnp.exp(raw - jnp.max(raw, axis=0, keepdims=True))
    p = e / jnp.sum(e, axis=0, keepdims=True)
    lb = jnp.sum(p[0:layer + 1, :], axis=0, keepdims=True) - p[0:1, :]

    ind = _head_indicator(w, jnp.bfloat16)
    r = lax.broadcasted_iota(jnp.int32, (rb, rb), 0)
    c = lax.broadcasted_iota(jnp.int32, (rb, rb), 1)
    tri = ((r >= c) & (r // c_len == c // c_len)).astype(jnp.bfloat16)
    row_full = lax.broadcasted_iota(jnp.int32, (c_len, w), 0)
    row_half = lax.broadcasted_iota(jnp.int32, (half, w), 0) + half
    first_head = lax.broadcasted_iota(jnp.int32, (HEAD_DIM, LANES), 1) < HEAD_DIM
    gn = gn_ref[...]

    st_ref[...] = jnp.zeros_like(st_ref)
    n_blocks = t_len // rb

    def front(i, slot):
        a_ref, qd_ref, kd_ref, dl_ref = slot
        r0 = pl.multiple_of(i * rb, rb)
        rows = pl.ds(r0, rb)
        q = _silu(proj_ref[0, rows, 0:w])
        z = proj_ref[0, rows, w:2 * w]

        sig, sig_neg = _sigmoid_pair(z)
        f = lb + (1.0 - lb) * sig
        logf = jnp.log(jnp.maximum(f, TINY))
        k = (1.0 - lb) * sig_neg
        b = jnp.zeros((rb, w), jnp.float32)
        rest = logf
        for _ in range(3):
            piece = rest.astype(jnp.bfloat16)
            rest = rest - piece.astype(jnp.float32)
            b = b + jnp.dot(tri, piece, preferred_element_type=jnp.float32)
        qd_ref[...] = (q * jnp.exp(b)).astype(jnp.bfloat16)
        for ch in range(n_chunks):
            lo = ch * c_len
            b_c = b[lo:lo + c_len, :]
            b_last = b_c[c_len - 1:c_len, :]
            kd_ref[lo:lo + c_len, :] = (k[lo:lo + c_len, :] * jnp.exp(b_last - b_c)
                                        ).astype(jnp.bfloat16)
            dl_ref[ch:ch + 1, :] = jnp.exp(b_last)

        parts = []
        for ch in range(n_chunks):
            lo = ch * c_len
            for s in range(c_len):
                top = lo if s < half else lo + half
                bt = b[top:lo + c_len, :]
                dec = jnp.exp(bt - b[lo + s:lo + s + 1, :])
                if s != 0 and s != half:
                    dec = jnp.where((row_full if s < half else row_half) >= s, dec, 0.0)
                parts.append(q[top:lo + c_len, :] * (k[lo + s:lo + s + 1, :] * dec))
        a = jnp.concatenate(parts, axis=0).astype(jnp.bfloat16)
        a_ref[...] = jnp.dot(a, ind, preferred_element_type=jnp.float32)

    def back(i, slot):
        a_ref, qd_ref, kd_ref, dl_ref = slot
        r0 = pl.multiple_of(i * rb, rb)
        rows = pl.ds(r0, rb)
        v = proj_ref[0, rows, 2 * w:3 * w]
        gate = proj_ref[0, rows, 3 * w:4 * w]
        v16 = v.astype(jnp.bfloat16)
        outs = []
        off = 0
        for ch in range(n_chunks):
            lo = ch * c_len
            o_top = jnp.zeros((half, w), jnp.float32)
            o_bot = jnp.zeros((half, w), jnp.float32)
            for s in range(c_len):
                vs = v[lo + s:lo + s + 1, :]
                if s < half:
                    o_top = o_top + a_ref[off:off + half, :] * vs
                    o_bot = o_bot + a_ref[off + half:off + c_len, :] * vs
                    off += c_len
                else:
                    o_bot = o_bot + a_ref[off:off + half, :] * vs
                    off += half

            dec_state = dl_ref[ch:ch + 1, :]
            inter = []
            for pr in range(pairs):
                lanes = slice(pr * LANES, (pr + 1) * LANES)
                st = st_ref[pr]
                wmat = jnp.concatenate([jnp.where(first_head, st, 0.0),
                                        jnp.where(first_head, 0.0, st)], axis=0).astype(jnp.bfloat16)
                inter.append(lax.dot_general(qd_ref[lo:lo + c_len, lanes], wmat,
                                             (((1,), (1,)), ((), ())),
                                             preferred_element_type=jnp.float32))
                upd = lax.dot_general(v16[lo:lo + c_len, lanes], kd_ref[lo:lo + c_len, lanes],
                                      (((0,), (0,)), ((), ())),
                                      preferred_element_type=jnp.float32)
                st_ref[pr] = st * dec_state[:, lanes] + jnp.where(
                    first_head, upd[0:HEAD_DIM, :], upd[HEAD_DIM:, :])
            outs.append(jnp.concatenate([o_top, o_bot], axis=0) + jnp.concatenate(inter, axis=1))
        o = jnp.concatenate(outs, axis=0)

        ms = jnp.dot((o * o).astype(jnp.bfloat16), ind,
                     preferred_element_type=jnp.float32) * (1.0 / HEAD_DIM)
        y = o * lax.rsqrt(ms + NORM_EPS) * gn
        o_ref[0, rows, :] = y * _silu(gate)

    slot_a, slot_b = slots[0:4], slots[4:8]
    front(0, slot_a)

    def two_blocks(j, carry):
        i0 = 2 * j
        front(i0 + 1, slot_b)
        back(i0, slot_a)
        front(jnp.minimum(i0 + 2, n_blocks - 1), slot_a)
        back(i0 + 1, slot_b)
        return carry

    lax.fori_loop(0, n_blocks // 2, two_blocks, 0)


def _hgrn(pa, lower_bounds, gn, layer):
    b, t, _ = pa.shape
    return pl.pallas_call(
        functools.partial(_hgrn_kernel, layer=layer),
        grid=(b,),
        in_specs=[
            pl.BlockSpec((DEPTH, HGRN_WIDTH), lambda i: (0, 0)),
            pl.BlockSpec((1, HGRN_WIDTH), lambda i: (0, 0)),
            pl.BlockSpec((1, t, 4 * HGRN_WIDTH), lambda i: (i, 0, COL_HGRN // (4 * HGRN_WIDTH))),
        ],
        out_specs=pl.BlockSpec((1, t, HGRN_WIDTH), lambda i: (i, 0, 0)),
        out_shape=jax.ShapeDtypeStruct((b, t, HGRN_WIDTH), jnp.float32),
        scratch_shapes=[pltpu.VMEM((HGRN_WIDTH // LANES, HEAD_DIM, LANES), jnp.float32)] + 2 * [
            pltpu.VMEM((HGRN_BLOCK * 3 * HGRN_CHUNK // 4, HGRN_WIDTH), jnp.float32),
            pltpu.VMEM((HGRN_BLOCK, HGRN_WIDTH), jnp.bfloat16),
            pltpu.VMEM((HGRN_BLOCK, HGRN_WIDTH), jnp.bfloat16),
            pltpu.VMEM((max(SUBLANES, HGRN_BLOCK // HGRN_CHUNK), HGRN_WIDTH), jnp.float32),
        ],
        compiler_params=pltpu.CompilerParams(
            dimension_semantics=("arbitrary",), vmem_limit_bytes=VMEM_LIMIT),
        name="hgrn2",
    )(lower_bounds, gn, pa)


def _pool_kernel(proj_ref, w_ref, scale_ref, o_ref, pad_ref):
    t_len = proj_ref.shape[1]
    w = POOL_WIDTH
    pad_ref[0:POOL_HALO, :] = jnp.zeros((POOL_HALO, w), jnp.float32)
    pad_ref[POOL_HALO:POOL_HALO + t_len, :] = proj_ref[0, :, 0:w]
    lane = lax.broadcasted_iota(jnp.int32, (POOL_ROWS, w), 1)
    row = lax.broadcasted_iota(jnp.int32, (POOL_ROWS, w), 0)
    group = lane // HEAD_DIM
    wb = w_ref[...]
    scale = scale_ref[...]

    def body(i, carry):
        r0 = pl.multiple_of(i * POOL_ROWS, POOL_ROWS)
        x = pad_ref[pl.ds(r0, POOL_ROWS + POOL_HALO), :]
        s2 = x + pltpu.roll(x, 1, axis=0)
        s4 = s2 + pltpu.roll(s2, 2, axis=0)
        s8 = s4 + pltpu.roll(s4, 4, axis=0)
        s16 = s8 + pltpu.roll(s8, 8, axis=0)
        tpos = (row + r0 + 1).astype(jnp.float32)
        pooled = None
        for gi, (win, sm) in enumerate(zip(POOL_WINDOWS, (s2, s4, s8, s16))):
            val = sm[POOL_HALO:, :] / jnp.minimum(tpos, float(win))
            pooled = val if pooled is None else jnp.where(group == gi, val, pooled)
        u = x[POOL_HALO:, :]
        pooled = pooled - u
        rows = pl.ds(r0, POOL_ROWS)
        mixed = jnp.dot(pooled.astype(jnp.bfloat16), wb, preferred_element_type=jnp.float32)
        o_ref[0, rows, :] = (mixed * scale) * _silu(proj_ref[0, rows, w:2 * w])
        return carry

    lax.fori_loop(0, t_len // POOL_ROWS, body, 0)


def _pool(pa, w_blockdiag, scale):
    b, t, _ = pa.shape
    return pl.pallas_call(
        _pool_kernel,
        grid=(b,),
        in_specs=[
            pl.BlockSpec((1, t, 2 * POOL_WIDTH), lambda i: (i, 0, COL_POOL // (2 * POOL_WIDTH))),
            pl.BlockSpec((POOL_WIDTH, POOL_WIDTH), lambda i: (0, 0)),
            pl.BlockSpec((1, POOL_WIDTH), lambda i: (0, 0)),
        ],
        out_specs=pl.BlockSpec((1, t, POOL_WIDTH), lambda i: (i, 0, 0)),
        out_shape=jax.ShapeDtypeStruct((b, t, POOL_WIDTH), jnp.float32),
        scratch_shapes=[pltpu.VMEM((t + POOL_HALO, POOL_WIDTH), jnp.float32)],
        compiler_params=pltpu.CompilerParams(dimension_semantics=("arbitrary",)),
        name="pool_mixer",
    )(pa, w_blockdiag, scale)


def _fox_kernel(q_ref, k_ref, v_ref, g_ref, qx_ref, kx_ref, o_ref, kcat_ref, vaug_ref, acc_ref,
                st_ref):
    pair = pl.program_id(1)
    qi = pl.program_id(2)
    bq = ATT_BLOCK
    lane = lax.broadcasted_iota(jnp.int32, (bq, LANES), 1)

    @pl.when(qi == 0)
    def _():
        t_len = k_ref.shape[1]
        kcat_ref[:, 0:LANES] = k_ref[0]
        kcat_ref[:, LANES:2 * LANES] = kx_ref[0]
        vt = v_ref[0].astype(jnp.float32).T
        row_t = lax.broadcasted_iota(jnp.int32, (LANES, t_len), 0)
        vaug_ref[0] = jnp.where(row_t < HEAD_DIM, vt, 1.0).astype(jnp.bfloat16)
        vaug_ref[1] = jnp.where(row_t >= HEAD_DIM, vt, 1.0).astype(jnp.bfloat16)

    qf = q_ref[0].astype(jnp.float32) * (HEAD_DIM ** -0.5 * LOG2E)
    qx = qx_ref[0]
    qcat = []
    for h2 in range(2):
        qh = jnp.where((lane // HEAD_DIM) == h2, qf, 0.0).astype(jnp.bfloat16)
        qxh = jnp.where((lane // BIAS_SLOT) == pair * 2 + h2, qx, jnp.zeros_like(qx))
        qcat.append(jnp.concatenate([qh, qxh], axis=1))
    acc_ref[...] = jnp.zeros_like(acc_ref)

    qt = ATT_QUERY_TILE
    units = [(h2, c) for h2 in range(2) for c in range(bq // qt)]
    qtile = {(h2, c): qcat[h2][c * qt:(c + 1) * qt, :] for h2, c in units}

    def scores(slot, k0, unit, nk):
        st_ref[slot, 0:nk, :] = lax.dot_general(
            kcat_ref[pl.ds(k0, nk), :], qtile[unit], (((1,), (1,)), ((), ())),
            preferred_element_type=jnp.float32)

    def consume(slot, k0, unit, nk, m_prev, diagonal):
        h2, c = unit
        st = st_ref[slot, 0:nk, :]
        if diagonal:
            key = lax.broadcasted_iota(jnp.int32, (nk, qt), 0)
            query = lax.broadcasted_iota(jnp.int32, (nk, qt), 1) + c * qt
            st = jnp.where(query >= key, st, MASK_VALUE)
        m_new = jnp.maximum(m_prev, jnp.max(st, axis=0, keepdims=True))
        pt = jnp.exp2(st - m_new).astype(jnp.bfloat16)
        alpha = jnp.exp2(m_prev - m_new)
        cols = slice(c * qt, (c + 1) * qt)
        acc_ref[h2, :, cols] = acc_ref[h2, :, cols] * alpha + jnp.dot(
            vaug_ref[h2, :, pl.ds(k0, nk)], pt, preferred_element_type=jnp.float32)
        return m_new

    def full_block(j, ms):
        k0 = pl.multiple_of(j * bq, bq)
        ms = list(ms)
        for u, unit in enumerate(units):
            if u + 1 < len(units):
                scores((u + 1) % 2, k0, units[u + 1], bq)
            else:
                scores((u + 1) % 2, k0 + bq, units[0], bq)
            ms[u] = consume(u % 2, k0, unit, bq, ms[u], False)
        return tuple(ms)

    def diag_keys(unit):
        return (unit[1] + 1) * qt

    scores(0, 0, units[0], bq)
    m_init = (jnp.full((1, qt), MASK_VALUE, jnp.float32),) * len(units)
    ms = list(lax.fori_loop(0, qi, full_block, m_init))
    k0 = pl.multiple_of(qi * bq, bq)
    for u, unit in enumerate(units):
        if u + 1 < len(units):
            scores((u + 1) % 2, k0, units[u + 1], diag_keys(units[u + 1]))
        ms[u] = consume(u % 2, k0, unit, diag_keys(unit), ms[u], True)

    a0 = acc_ref[0]
    a1 = acc_ref[1]
    ot = jnp.concatenate([a0[0:HEAD_DIM, :] / a0[HEAD_DIM:HEAD_DIM + 1, :],
                          a1[HEAD_DIM:, :] / a1[0:1, :]], axis=0)
    o_ref[0] = ot.T * _silu(g_ref[0].astype(jnp.float32))


def _fox(pc, qx, kx):
    b, t, _ = pc.shape
    pairs = FOX_HEADS // 2
    blocks_per_section = FOX_WIDTH // LANES
    q_spec = lambda sec: pl.BlockSpec(
        (1, ATT_BLOCK, LANES), lambda i, p, q: (i, q, sec * blocks_per_section + p))
    kv_spec = lambda sec: pl.BlockSpec(
        (1, t, LANES), lambda i, p, q: (i, 0, sec * blocks_per_section + p))
    return pl.pallas_call(
        _fox_kernel,
        grid=(b, pairs, t // ATT_BLOCK),
        in_specs=[
            q_spec(0), kv_spec(1), kv_spec(2), q_spec(3),
            pl.BlockSpec((1, ATT_BLOCK, LANES), lambda i, p, q: (i, q, 0)),
            pl.BlockSpec((1, t, LANES), lambda i, p, q: (i, 0, 0)),
        ],
        out_specs=pl.BlockSpec((1, ATT_BLOCK, LANES), lambda i, p, q: (i, q, p)),
        out_shape=jax.ShapeDtypeStruct((b, t, FOX_WIDTH), jnp.float32),
        scratch_shapes=[
            pltpu.VMEM((t, 2 * LANES), jnp.bfloat16),
            pltpu.VMEM((2, LANES, t), jnp.bfloat16),
            pltpu.VMEM((2, LANES, ATT_BLOCK), jnp.float32),
            pltpu.VMEM((2, ATT_BLOCK, ATT_QUERY_TILE), jnp.float32),
        ],
        compiler_params=pltpu.CompilerParams(
            dimension_semantics=("arbitrary", "arbitrary", "arbitrary"),
            vmem_limit_bytes=VMEM_LIMIT),
        name="fox_attention",
    )(pc, pc, pc, pc, qx, kx)


def _outproj_kernel(x_ref, a_ref, b_ref, c_ref, w_ref, g_ref, o_ref):
    mixed = jnp.concatenate([a_ref[...], b_ref[...], c_ref[...]], axis=-1).astype(jnp.bfloat16)
    y = jnp.dot(mixed, w_ref[...], preferred_element_type=jnp.float32)
    ms = jnp.mean(y * y, axis=-1, keepdims=True)
    o_ref[...] = x_ref[...] + y * lax.rsqrt(ms + NORM_EPS) * g_ref[...]


def _outproj(x2d, o_a, o_b, o_c, w, g):
    m = x2d.shape[0]
    row_spec = lambda width: pl.BlockSpec((PROJ_ROWS, width), lambda i: (i, 0))
    return pl.pallas_call(
        _outproj_kernel,
        grid=(m // PROJ_ROWS,),
        in_specs=[
            row_spec(D_MODEL), row_spec(HGRN_WIDTH), row_spec(POOL_WIDTH), row_spec(FOX_WIDTH),
            pl.BlockSpec((D_MODEL, D_MODEL), lambda i: (0, 0)),
            pl.BlockSpec((1, D_MODEL), lambda i: (0, 0)),
        ],
        out_specs=row_spec(D_MODEL),
        out_shape=jax.ShapeDtypeStruct((m, D_MODEL), jnp.float32),
        compiler_params=pltpu.CompilerParams(
            dimension_semantics=("arbitrary",), vmem_limit_bytes=VMEM_LIMIT),
        name="outproj",
    )(x2d, o_a, o_b, o_c, w, g)


def kernel(x, lower_bounds, pre_norm_g, w_in, hgrn_norm_g, fox_f_bias, pool_w, pool_scale, w_out,
           post_norm_g):
    b, t, d = x.shape
    m = b * t
    x2d = x.reshape(m, d)
    for layer in range(DEPTH):
        w_in_l = jnp.pad(w_in[layer], ((0, 0), (0, IN_WIDTH_PAD - IN_WIDTH))).astype(jnp.bfloat16)
        bias_pad = jnp.pad(fox_f_bias[layer], (0, LANES - FOX_HEADS)).reshape(1, LANES)
        pool_bd = jax.scipy.linalg.block_diag(*pool_w[layer]).astype(jnp.bfloat16)

        pa, pc, pf = _inproj(x2d, pre_norm_g[layer].reshape(1, d), w_in_l)
        pa = pa.reshape(b, t, COL_Q)
        pc = pc.reshape(b, t, COL_FC - COL_Q)
        qx, kx = _forget_cumsum(pf.reshape(b, t, LANES), bias_pad)
        o_a = _hgrn(pa, lower_bounds, hgrn_norm_g[layer].reshape(1, HGRN_WIDTH), layer)
        o_b = _pool(pa, pool_bd, pool_scale[layer].reshape(1, POOL_WIDTH))
        o_c = _fox(pc, qx, kx)
        x2d = _outproj(x2d, o_a.reshape(m, HGRN_WIDTH), o_b.reshape(m, POOL_WIDTH),
                       o_c.reshape(m, FOX_WIDTH), w_out[layer].astype(jnp.bfloat16),
                       post_norm_g[layer].reshape(1, d))
    return x2d.reshape(b, t, d)
```

```python
import functools
import math

import jax
import jax.numpy as jnp
from jax import lax
from jax.experimental import pallas as pl
from jax.experimental.pallas import tpu as pltpu

D_MODEL = 1024
DEPTH = 2
NORM_EPS = 1e-6
MASK_VALUE = -1e30
TINY = 1e-30

HGRN_HEADS = 4
HGRN_WIDTH = 256
HEAD_DIM = 64
POOL_WINDOWS = (2, 4, 8, 16)
POOL_WIDTH = 256
FOX_HEADS = 8
FOX_WIDTH = 512
IN_WIDTH = 3592

LANES = 128
SUBLANES = 8
IN_WIDTH_PAD = 29 * LANES
VMEM_LIMIT = 56 * 1024 * 1024

COL_HGRN = 0
COL_POOL = 1024
COL_Q = 1536
COL_FC = 3584

HGRN_CHUNK = 16
HGRN_BLOCK = 64
PROJ_ROWS = 512
ATT_BLOCK = 512
ATT_DIAG_TILE = 256
CUM_BLOCK = 256
POOL_ROWS = 128
POOL_HALO = 16

LOG2E = math.log2(math.e)
BIAS_SLOT = 8


def _sigmoid_pair(z):
    e = jnp.exp(-jnp.abs(z))
    r = 1.0 / (1.0 + e)
    big, small = r, e * r
    pos = z >= 0
    return jnp.where(pos, big, small), jnp.where(pos, small, big)


def _silu(x):
    s, _ = _sigmoid_pair(x)
    return x * s


def _head_indicator(n, dtype):
    r = lax.broadcasted_iota(jnp.int32, (n, n), 0) // HEAD_DIM
    c = lax.broadcasted_iota(jnp.int32, (n, n), 1) // HEAD_DIM
    return (r == c).astype(dtype)


def _inproj_kernel(x_ref, g_ref, w_ref, pa_ref, pc_ref, pf_ref):
    x = x_ref[...]
    ms = jnp.mean(x * x, axis=-1, keepdims=True)
    h = (x * lax.rsqrt(ms + NORM_EPS) * g_ref[...]).astype(jnp.bfloat16)
    pa_ref[...] = jnp.dot(h, w_ref[:, 0:COL_Q], preferred_element_type=jnp.float32)
    pc_ref[...] = jnp.dot(h, w_ref[:, COL_Q:COL_FC],
                          preferred_element_type=jnp.float32).astype(jnp.bfloat16)
    pf_ref[...] = jnp.dot(h, w_ref[:, COL_FC:IN_WIDTH_PAD], preferred_element_type=jnp.float32)


def _inproj(x2d, g, w):
    m = x2d.shape[0]
    widths = (COL_Q, COL_FC - COL_Q, IN_WIDTH_PAD - COL_FC)
    dtypes = (jnp.float32, jnp.bfloat16, jnp.float32)
    return pl.pallas_call(
        _inproj_kernel,
        grid=(m // PROJ_ROWS,),
        in_specs=[
            pl.BlockSpec((PROJ_ROWS, D_MODEL), lambda i: (i, 0)),
            pl.BlockSpec((1, D_MODEL), lambda i: (0, 0)),
            pl.BlockSpec((D_MODEL, IN_WIDTH_PAD), lambda i: (0, 0)),
        ],
        out_specs=[pl.BlockSpec((PROJ_ROWS, wd), lambda i: (i, 0)) for wd in widths],
        out_shape=[jax.ShapeDtypeStruct((m, wd), dt) for wd, dt in zip(widths, dtypes)],
        compiler_params=pltpu.CompilerParams(
            dimension_semantics=("arbitrary",), vmem_limit_bytes=VMEM_LIMIT),
        name="inproj",
    )(x2d, g, w)


def _bias_placement(part, negate_offset):
    r = lax.broadcasted_iota(jnp.int32, (LANES, LANES), 0)
    c = lax.broadcasted_iota(jnp.int32, (LANES, LANES), 1)
    return ((c == r * BIAS_SLOT + part + negate_offset) & (r < FOX_HEADS)).astype(jnp.bfloat16)


def _cumsum_kernel(f_ref, bias_ref, qx_ref, kx_ref):
    t_len = f_ref.shape[1]
    r = lax.broadcasted_iota(jnp.int32, (CUM_BLOCK, CUM_BLOCK), 0)
    c = lax.broadcasted_iota(jnp.int32, (CUM_BLOCK, CUM_BLOCK), 1)
    tri = (r >= c).astype(jnp.bfloat16)
    lane = lax.broadcasted_iota(jnp.int32, (CUM_BLOCK, LANES), 1)
    slot = lane % BIAS_SLOT
    used = lane < FOX_HEADS * BIAS_SLOT
    ones_q = (used & (slot >= 3) & (slot < 6)).astype(jnp.float32)
    ones_k = (used & (slot < 3)).astype(jnp.float32)
    place_q = [_bias_placement(part, 0) for part in range(3)]
    place_k = [_bias_placement(part, 3) for part in range(3)]
    carry = jnp.zeros((1, LANES), jnp.float32)
    for blk in range(t_len // CUM_BLOCK):
        rows = pl.ds(blk * CUM_BLOCK, CUM_BLOCK)
        v = f_ref[0, rows, :] + bias_ref[...]
        logf = jnp.minimum(v, 0.0) - jnp.log1p(jnp.exp(-jnp.abs(v)))
        cs = carry
        for _ in range(3):
            piece = logf.astype(jnp.bfloat16)
            logf = logf - piece.astype(jnp.float32)
            cs = cs + jnp.dot(tri, piece, preferred_element_type=jnp.float32)
        carry = cs[CUM_BLOCK - 1:CUM_BLOCK, :]
        rest = cs * LOG2E
        spread_q = ones_q
        spread_k = ones_k
        for part in range(3):
            piece = rest.astype(jnp.bfloat16)
            rest = rest - piece.astype(jnp.float32)
            spread_q = spread_q + jnp.dot(piece, place_q[part], preferred_element_type=jnp.float32)
            spread_k = spread_k - jnp.dot(piece, place_k[part], preferred_element_type=jnp.float32)
        qx_ref[0, rows, :] = spread_q.astype(jnp.bfloat16)
        kx_ref[0, rows, :] = spread_k.astype(jnp.bfloat16)


def _forget_cumsum(pf, bias_pad):
    b, t, _ = pf.shape
    spec = pl.BlockSpec((1, t, LANES), lambda i: (i, 0, 0))
    return pl.pallas_call(
        _cumsum_kernel,
        grid=(b,),
        in_specs=[spec, pl.BlockSpec((1, LANES), lambda i: (0, 0))],
        out_specs=[spec, spec],
        out_shape=[jax.ShapeDtypeStruct((b, t, LANES), jnp.bfloat16)] * 2,
        compiler_params=pltpu.CompilerParams(dimension_semantics=("arbitrary",)),
        name="forget_cumsum",
    )(pf, bias_pad)


def _hgrn_kernel(lb_ref, gn_ref, proj_ref, o_ref, st_ref, *slots, layer):
    t_len = proj_ref.shape[1]
    w = HGRN_WIDTH
    c_len = HGRN_CHUNK
    half = c_len // 2
    rb = HGRN_BLOCK
    n_chunks = rb // c_len
    pairs = w // LANES

    raw = lb_ref[...]
    e = jnp.exp(raw - jnp.max(raw, axis=0, keepdims=True))
    p = e / jnp.sum(e, axis=0, keepdims=True)
    lb = jnp.sum(p[0:layer + 1, :], axis=0, keepdims=True) - p[0:1, :]

    ind = _head_indicator(w, jnp.bfloat16)
    r = lax.broadcasted_iota(jnp.int32, (rb, rb), 0)
    c = lax.broadcasted_iota(jnp.int32, (rb, rb), 1)
    tri = ((r >= c) & (r // c_len == c // c_len)).astype(jnp.bfloat16)
    row_tile = lax.broadcasted_iota(jnp.int32, (half, w), 0)
    first_head = lax.broadcasted_iota(jnp.int32, (HEAD_DIM, LANES), 1) < HEAD_DIM
    gn = gn_ref[...]

    st_ref[...] = jnp.zeros_like(st_ref)
    n_blocks = t_len // rb
    ROW_B, ROW_K, ROW_V = range(3)

    def put_rows(rows_ref, which, x):
        for pr in range(pairs):
            rows_ref[which, pr] = x[:, pr * LANES:(pr + 1) * LANES]

    def row_tile_of(rows_ref, which, r):
        return jnp.concatenate([rows_ref[which, pr, pl.ds(r, half, stride=0), :]
                                for pr in range(pairs)], axis=1)

    def front(i, slot):
        a_ref, qd_ref, kd_ref, dl_ref, rows_ref = slot
        r0 = pl.multiple_of(i * rb, rb)
        rows = pl.ds(r0, rb)
        q = _silu(proj_ref[0, rows, 0:w])
        z = proj_ref[0, rows, w:2 * w]

        sig, sig_neg = _sigmoid_pair(z)
        f = lb + (1.0 - lb) * sig
        logf = jnp.log(jnp.maximum(f, TINY)) * LOG2E
        k = (1.0 - lb) * sig_neg
        b = jnp.zeros((rb, w), jnp.float32)
        rest = logf
        for _ in range(3):
            piece = rest.astype(jnp.bfloat16)
            rest = rest - piece.astype(jnp.float32)
            b = b + jnp.dot(tri, piece, preferred_element_type=jnp.float32)
        put_rows(rows_ref, ROW_B, b)
        put_rows(rows_ref, ROW_K, k)
        put_rows(rows_ref, ROW_V, proj_ref[0, rows, 2 * w:3 * w])
        qd_ref[...] = (q * jnp.exp2(b)).astype(jnp.bfloat16)
        for ch in range(n_chunks):
            lo = ch * c_len
            b_last = row_tile_of(rows_ref, ROW_B, lo + c_len - 1)
            for t0 in (lo, lo + half):
                kd_ref[t0:t0 + half, :] = (k[t0:t0 + half, :] * jnp.exp2(b_last - b[t0:t0 + half, :])
                                           ).astype(jnp.bfloat16)
            dl_ref[ch:ch + 1, :] = jnp.exp2(b_last[0:1, :])

        parts = []
        for ch in range(n_chunks):
            lo = ch * c_len
            for s in range(c_len):
                bs = row_tile_of(rows_ref, ROW_B, lo + s)
                ks = row_tile_of(rows_ref, ROW_K, lo + s)
                for tl in range(s // half, c_len // half):
                    t0 = lo + tl * half
                    dec = jnp.exp2(b[t0:t0 + half, :] - bs)
                    if tl * half < s:
                        dec = jnp.where(row_tile + tl * half >= s, dec, 0.0)
                    parts.append(q[t0:t0 + half, :] * (ks * dec))
        a = jnp.concatenate(parts, axis=0).astype(jnp.bfloat16)
        a_ref[...] = jnp.dot(a, ind, preferred_element_type=jnp.float32)

    def back(i, slot):
        a_ref, qd_ref, kd_ref, dl_ref, rows_ref = slot
        r0 = pl.multiple_of(i * rb, rb)
        rows = pl.ds(r0, rb)
        gate = proj_ref[0, rows, 3 * w:4 * w]
        v16 = proj_ref[0, rows, 2 * w:3 * w].astype(jnp.bfloat16)
        outs = []
        off = 0
        for ch in range(n_chunks):
            lo = ch * c_len
            o_tiles = [jnp.zeros((half, w), jnp.float32) for _ in range(c_len // half)]
            for s in range(c_len):
                vs = row_tile_of(rows_ref, ROW_V, lo + s)
                for tl in range(s // half, c_len // half):
                    o_tiles[tl] = o_tiles[tl] + a_ref[off:off + half, :] * vs
                    off += half

            dec_state = dl_ref[ch:ch + 1, :]
            inter = []
            for pr in range(pairs):
                lanes = slice(pr * LANES, (pr + 1) * LANES)
                st = st_ref[pr]
                wmat = jnp.concatenate([jnp.where(first_head, st, 0.0),
                                        jnp.where(first_head, 0.0, st)], axis=0).astype(jnp.bfloat16)
                inter.append(lax.dot_general(qd_ref[lo:lo + c_len, lanes], wmat,
                                             (((1,), (1,)), ((), ())),
                                             preferred_element_type=jnp.float32))
                upd = lax.dot_general(v16[lo:lo + c_len, lanes], kd_ref[lo:lo + c_len, lanes],
                                      (((0,), (0,)), ((), ())),
                                      preferred_element_type=jnp.float32)
                st_ref[pr] = st * dec_state[:, lanes] + jnp.where(
                    first_head, upd[0:HEAD_DIM, :], upd[HEAD_DIM:, :])
            outs.append(jnp.concatenate(o_tiles, axis=0) + jnp.concatenate(inter, axis=1))
        o = jnp.concatenate(outs, axis=0)

        ms = jnp.dot((o * o).astype(jnp.bfloat16), ind,
                     preferred_element_type=jnp.float32) * (1.0 / HEAD_DIM)
        y = o * lax.rsqrt(ms + NORM_EPS) * gn
        o_ref[0, rows, :] = (y * _silu(gate)).astype(o_ref.dtype)

    slot_a, slot_b = slots[0:5], slots[5:10]
    front(0, slot_a)

    def two_blocks(j, carry):
        i0 = 2 * j
        front(i0 + 1, slot_b)
        back(i0, slot_a)
        front(jnp.minimum(i0 + 2, n_blocks - 1), slot_a)
        back(i0 + 1, slot_b)
        return carry

    lax.fori_loop(0, n_blocks // 2, two_blocks, 0)


def _hgrn(pa, lower_bounds, gn, layer):
    b, t, _ = pa.shape
    return pl.pallas_call(
        functools.partial(_hgrn_kernel, layer=layer),
        grid=(b,),
        in_specs=[
            pl.BlockSpec((DEPTH, HGRN_WIDTH), lambda i: (0, 0)),
            pl.BlockSpec((1, HGRN_WIDTH), lambda i: (0, 0)),
            pl.BlockSpec((1, t, 4 * HGRN_WIDTH), lambda i: (i, 0, COL_HGRN // (4 * HGRN_WIDTH))),
        ],
        out_specs=pl.BlockSpec((1, t, HGRN_WIDTH), lambda i: (i, 0, 0)),
        out_shape=jax.ShapeDtypeStruct((b, t, HGRN_WIDTH), jnp.bfloat16),
        scratch_shapes=[pltpu.VMEM((HGRN_WIDTH // LANES, HEAD_DIM, LANES), jnp.float32)] + 2 * [
            pltpu.VMEM((HGRN_BLOCK * 3 * HGRN_CHUNK // 4, HGRN_WIDTH), jnp.float32),
            pltpu.VMEM((HGRN_BLOCK, HGRN_WIDTH), jnp.bfloat16),
            pltpu.VMEM((HGRN_BLOCK, HGRN_WIDTH), jnp.bfloat16),
            pltpu.VMEM((max(SUBLANES, HGRN_BLOCK // HGRN_CHUNK), HGRN_WIDTH), jnp.float32),
            pltpu.VMEM((3, HGRN_WIDTH // LANES, HGRN_BLOCK, LANES), jnp.float32),
        ],
        compiler_params=pltpu.CompilerParams(
            dimension_semantics=("arbitrary",), vmem_limit_bytes=VMEM_LIMIT),
        name="hgrn2",
    )(lower_bounds, gn, pa)


def _pool_kernel(proj_ref, w_ref, scale_ref, o_ref, pad_ref):
    t_len = proj_ref.shape[1]
    w = POOL_WIDTH
    pad_ref[0:POOL_HALO, :] = jnp.zeros((POOL_HALO, w), jnp.float32)
    pad_ref[POOL_HALO:POOL_HALO + t_len, :] = proj_ref[0, :, 0:w]
    lane = lax.broadcasted_iota(jnp.int32, (POOL_ROWS, w), 1)
    row = lax.broadcasted_iota(jnp.int32, (POOL_ROWS, w), 0)
    group = lane // HEAD_DIM
    wb = w_ref[...]
    scale = scale_ref[...]

    def body(i, carry):
        r0 = pl.multiple_of(i * POOL_ROWS, POOL_ROWS)
        x = pad_ref[pl.ds(r0, POOL_ROWS + POOL_HALO), :]
        s2 = x + pltpu.roll(x, 1, axis=0)
        s4 = s2 + pltpu.roll(s2, 2, axis=0)
        s8 = s4 + pltpu.roll(s4, 4, axis=0)
        s16 = s8 + pltpu.roll(s8, 8, axis=0)
        tpos = (row + r0 + 1).astype(jnp.float32)
        pooled = None
        for gi, (win, sm) in enumerate(zip(POOL_WINDOWS, (s2, s4, s8, s16))):
            val = sm[POOL_HALO:, :] / jnp.minimum(tpos, float(win))
            pooled = val if pooled is None else jnp.where(group == gi, val, pooled)
        u = x[POOL_HALO:, :]
        pooled = pooled - u
        rows = pl.ds(r0, POOL_ROWS)
        mixed = jnp.dot(pooled.astype(jnp.bfloat16), wb, preferred_element_type=jnp.float32)
        o_ref[0, rows, :] = ((mixed * scale) * _silu(proj_ref[0, rows, w:2 * w])).astype(o_ref.dtype)
        return carry

    lax.fori_loop(0, t_len // POOL_ROWS, body, 0)


def _pool(pa, w_blockdiag, scale):
    b, t, _ = pa.shape
    return pl.pallas_call(
        _pool_kernel,
        grid=(b,),
        in_specs=[
            pl.BlockSpec((1, t, 2 * POOL_WIDTH), lambda i: (i, 0, COL_POOL // (2 * POOL_WIDTH))),
            pl.BlockSpec((POOL_WIDTH, POOL_WIDTH), lambda i: (0, 0)),
            pl.BlockSpec((1, POOL_WIDTH), lambda i: (0, 0)),
        ],
        out_specs=pl.BlockSpec((1, t, POOL_WIDTH), lambda i: (i, 0, 0)),
        out_shape=jax.ShapeDtypeStruct((b, t, POOL_WIDTH), jnp.bfloat16),
        scratch_shapes=[pltpu.VMEM((t + POOL_HALO, POOL_WIDTH), jnp.float32)],
        compiler_params=pltpu.CompilerParams(dimension_semantics=("arbitrary",)),
        name="pool_mixer",
    )(pa, w_blockdiag, scale)


def _fox_kernel(q_ref, k_ref, v_ref, g_ref, qx_ref, kx_ref, o_ref, kcat_ref, qcat_ref, vaug_ref,
                acc_ref, st_ref):
    pair = pl.program_id(1)
    t_len = k_ref.shape[1]
    bq = ATT_BLOCK
    tile = ATT_DIAG_TILE
    n_qblocks = t_len // bq

    kcat_ref[:, 0:LANES] = k_ref[0]
    kcat_ref[:, LANES:2 * LANES] = kx_ref[0]
    vt = v_ref[0].astype(jnp.float32).T
    row_t = lax.broadcasted_iota(jnp.int32, (LANES, t_len), 0)
    vaug_ref[0] = jnp.where(row_t < HEAD_DIM, vt, 1.0).astype(jnp.bfloat16)
    vaug_ref[1] = jnp.where(row_t >= HEAD_DIM, vt, 1.0).astype(jnp.bfloat16)
    lane = lax.broadcasted_iota(jnp.int32, (t_len, LANES), 1)
    qf = q_ref[0].astype(jnp.float32) * (HEAD_DIM ** -0.5 * LOG2E)
    qx = qx_ref[0]
    for h2 in range(2):
        qcat_ref[h2, :, 0:LANES] = jnp.where((lane // HEAD_DIM) == h2, qf, 0.0).astype(jnp.bfloat16)
        qcat_ref[h2, :, LANES:2 * LANES] = jnp.where(
            (lane // BIAS_SLOT) == pair * 2 + h2, qx, jnp.zeros_like(qx))
    acc_ref[...] = jnp.zeros_like(acc_ref)

    units = []
    for qb in range(n_qblocks):
        for kb in range(qb):
            units += [(qb, h2, qb * bq, bq, kb * bq, bq, False) for h2 in range(2)]
        for h2 in range(2):
            units += [(qb, h2, qb * bq + c * tile, tile, qb * bq, (c + 1) * tile, True)
                      for c in range(bq // tile)]

    def scores(slot, unit):
        _, h2, q0, nq, k0, nk, _ = unit
        st_ref[slot, 0:nk, 0:nq] = lax.dot_general(
            kcat_ref[k0:k0 + nk, :], qcat_ref[h2, q0:q0 + nq, :], (((1,), (1,)), ((), ())),
            preferred_element_type=jnp.float32)

    m_run = {(qb, h2, c): jnp.full((1, tile), MASK_VALUE, jnp.float32)
             for qb in range(n_qblocks) for h2 in range(2) for c in range(bq // tile)}

    def consume(slot, unit):
        qb, h2, q0, nq, k0, nk, masked = unit
        st = st_ref[slot, 0:nk, 0:nq]
        if masked:
            key = lax.broadcasted_iota(jnp.int32, (nk, nq), 0) + k0
            query = lax.broadcasted_iota(jnp.int32, (nk, nq), 1) + q0
            st = jnp.where(query >= key, st, MASK_VALUE)
        tiles = [(q0 - qb * bq) // tile + i for i in range(nq // tile)]
        m_prev = jnp.concatenate([m_run[(qb, h2, c)] for c in tiles], axis=1)
        m_new = jnp.maximum(m_prev, jnp.max(st, axis=0, keepdims=True))
        pt = jnp.exp2(st - m_new).astype(jnp.bfloat16)
        alpha = jnp.exp2(m_prev - m_new)
        cols = slice(q0 - qb * bq, q0 - qb * bq + nq)
        acc_ref[qb, h2, :, cols] = acc_ref[qb, h2, :, cols] * alpha + jnp.dot(
            vaug_ref[h2, :, k0:k0 + nk], pt, preferred_element_type=jnp.float32)
        for i, c in enumerate(tiles):
            m_run[(qb, h2, c)] = m_new[:, i * tile:(i + 1) * tile]

    def finish(qb):
        a0 = acc_ref[qb, 0]
        a1 = acc_ref[qb, 1]
        ot = jnp.concatenate([a0[0:HEAD_DIM, :] / a0[HEAD_DIM:HEAD_DIM + 1, :],
                              a1[HEAD_DIM:, :] / a1[0:1, :]], axis=0)
        rows = slice(qb * bq, (qb + 1) * bq)
        gate = _silu(g_ref[0, rows, :].astype(jnp.float32))
        o_ref[0, rows, :] = (ot.T * gate).astype(o_ref.dtype)

    scores(0, units[0])
    for u, unit in enumerate(units):
        if u + 1 < len(units):
            scores((u + 1) % 2, units[u + 1])
        consume(u % 2, unit)
        if u + 1 == len(units) or units[u + 1][0] != unit[0]:
            finish(unit[0])


def _fox(pc, qx, kx):
    b, t, _ = pc.shape
    pairs = FOX_HEADS // 2
    blocks_per_section = FOX_WIDTH // LANES
    section = lambda sec: pl.BlockSpec(
        (1, t, LANES), lambda i, p: (i, 0, sec * blocks_per_section + p))
    shared = pl.BlockSpec((1, t, LANES), lambda i, p: (i, 0, 0))
    return pl.pallas_call(
        _fox_kernel,
        grid=(b, pairs),
        in_specs=[section(0), section(1), section(2), section(3), shared, shared],
        out_specs=pl.BlockSpec((1, t, LANES), lambda i, p: (i, 0, p)),
        out_shape=jax.ShapeDtypeStruct((b, t, FOX_WIDTH), jnp.bfloat16),
        scratch_shapes=[
            pltpu.VMEM((t, 2 * LANES), jnp.bfloat16),
            pltpu.VMEM((2, t, 2 * LANES), jnp.bfloat16),
            pltpu.VMEM((2, LANES, t), jnp.bfloat16),
            pltpu.VMEM((t // ATT_BLOCK, 2, LANES, ATT_BLOCK), jnp.float32),
            pltpu.VMEM((2, ATT_BLOCK, ATT_BLOCK), jnp.float32),
        ],
        compiler_params=pltpu.CompilerParams(
            dimension_semantics=("arbitrary", "arbitrary"), vmem_limit_bytes=VMEM_LIMIT),
        name="fox_attention",
    )(pc, pc, pc, pc, qx, kx)


def _outproj_kernel(x_ref, a_ref, b_ref, c_ref, w_ref, g_ref, o_ref):
    mixed = jnp.concatenate([a_ref[...], b_ref[...], c_ref[...]], axis=-1)
    y = jnp.dot(mixed, w_ref[...], preferred_element_type=jnp.float32)
    ms = jnp.mean(y * y, axis=-1, keepdims=True)
    o_ref[...] = x_ref[...] + y * lax.rsqrt(ms + NORM_EPS) * g_ref[...]


def _outproj(x2d, o_a, o_b, o_c, w, g):
    m = x2d.shape[0]
    row_spec = lambda width: pl.BlockSpec((PROJ_ROWS, width), lambda i: (i, 0))
    return pl.pallas_call(
        _outproj_kernel,
        grid=(m // PROJ_ROWS,),
        in_specs=[
            row_spec(D_MODEL), row_spec(HGRN_WIDTH), row_spec(POOL_WIDTH), row_spec(FOX_WIDTH),
            pl.BlockSpec((D_MODEL, D_MODEL), lambda i: (0, 0)),
            pl.BlockSpec((1, D_MODEL), lambda i: (0, 0)),
        ],
        out_specs=row_spec(D_MODEL),
        out_shape=jax.ShapeDtypeStruct((m, D_MODEL), jnp.float32),
        compiler_params=pltpu.CompilerParams(
            dimension_semantics=("arbitrary",), vmem_limit_bytes=VMEM_LIMIT),
        name="outproj",
    )(x2d, o_a, o_b, o_c, w, g)


def kernel(x, lower_bounds, pre_norm_g, w_in, hgrn_norm_g, fox_f_bias, pool_w, pool_scale, w_out,
           post_norm_g):
    b, t, d = x.shape
    m = b * t
    x2d = x.reshape(m, d)
    for layer in range(DEPTH):
        w_in_l = jnp.pad(w_in[layer], ((0, 0), (0, IN_WIDTH_PAD - IN_WIDTH))).astype(jnp.bfloat16)
        bias_pad = jnp.pad(fox_f_bias[layer], (0, LANES - FOX_HEADS)).reshape(1, LANES)
        pool_bd = jax.scipy.linalg.block_diag(*pool_w[layer]).astype(jnp.bfloat16)

        pa, pc, pf = _inproj(x2d, pre_norm_g[layer].reshape(1, d), w_in_l)
        pa = pa.reshape(b, t, COL_Q)
        pc = pc.reshape(b, t, COL_FC - COL_Q)
        qx, kx = _forget_cumsum(pf.reshape(b, t, LANES), bias_pad)
        o_a = _hgrn(pa, lower_bounds, hgrn_norm_g[layer].reshape(1, HGRN_WIDTH), layer)
        o_b = _pool(pa, pool_bd, pool_scale[layer].reshape(1, POOL_WIDTH))
        o_c = _fox(pc, qx, kx)
        x2d = _outproj(x2d, o_a.reshape(m, HGRN_WIDTH), o_b.reshape(m, POOL_WIDTH),
                       o_c.reshape(m, FOX_WIDTH), w_out[layer].astype(jnp.bfloat16),
                       post_norm_g[layer].reshape(1, d))
    return x2d.reshape(b, t, d)
```

```python
import functools
import math

import jax
import jax.numpy as jnp
from jax import lax
from jax.experimental import pallas as pl
from jax.experimental.pallas import tpu as pltpu

D_MODEL = 1024
DEPTH = 2
NORM_EPS = 1e-6
MASK_VALUE = -1e30
TINY = 1e-30

HGRN_HEADS = 4
HGRN_WIDTH = 256
HEAD_DIM = 64
POOL_WINDOWS = (2, 4, 8, 16)
POOL_WIDTH = 256
FOX_HEADS = 8
FOX_WIDTH = 512
IN_WIDTH = 3592

LANES = 128
SUBLANES = 8
IN_WIDTH_PAD = 29 * LANES
VMEM_LIMIT = 56 * 1024 * 1024

COL_HGRN = 0
COL_POOL = 1024
COL_Q = 1536
COL_FC = 3584

HGRN_CHUNK = 16
HGRN_BLOCK = 128
HGRN_UNROLL = 4
PROJ_ROWS = 512
ATT_BLOCK = 512
ATT_DIAG_TILE = 256
ATT_SLOTS = 4
ATT_ONES_ROWS = 16
CUM_BLOCK = 256
POOL_HALO = 16

LOG2E = math.log2(math.e)
BIAS_SLOT = 8


def _sigmoid_pair(z):
    e = jnp.exp(-jnp.abs(z))
    r = 1.0 / (1.0 + e)
    big, small = r, e * r
    pos = z >= 0
    return jnp.where(pos, big, small), jnp.where(pos, small, big)


def _silu(x):
    s, _ = _sigmoid_pair(x)
    return x * s


def _head_indicator(n, dtype):
    r = lax.broadcasted_iota(jnp.int32, (n, n), 0) // HEAD_DIM
    c = lax.broadcasted_iota(jnp.int32, (n, n), 1) // HEAD_DIM
    return (r == c).astype(dtype)


def _inproj_kernel(x_ref, g_ref, w_ref, pa_ref, pc_ref, pf_ref):
    x = x_ref[...]
    ms = jnp.mean(x * x, axis=-1, keepdims=True)
    h = (x * lax.rsqrt(ms + NORM_EPS) * g_ref[...]).astype(jnp.bfloat16)
    pa_ref[...] = jnp.dot(h, w_ref[:, 0:COL_Q], preferred_element_type=jnp.float32)
    pc_ref[...] = jnp.dot(h, w_ref[:, COL_Q:COL_FC],
                          preferred_element_type=jnp.float32).astype(jnp.bfloat16)
    pf_ref[...] = jnp.dot(h, w_ref[:, COL_FC:IN_WIDTH_PAD], preferred_element_type=jnp.float32)


def _inproj(x2d, g, w):
    m = x2d.shape[0]
    widths = (COL_Q, COL_FC - COL_Q, IN_WIDTH_PAD - COL_FC)
    dtypes = (jnp.float32, jnp.bfloat16, jnp.float32)
    return pl.pallas_call(
        _inproj_kernel,
        grid=(m // PROJ_ROWS,),
        in_specs=[
            pl.BlockSpec((PROJ_ROWS, D_MODEL), lambda i: (i, 0)),
            pl.BlockSpec((1, D_MODEL), lambda i: (0, 0)),
            pl.BlockSpec((D_MODEL, IN_WIDTH_PAD), lambda i: (0, 0)),
        ],
        out_specs=[pl.BlockSpec((PROJ_ROWS, wd), lambda i: (i, 0)) for wd in widths],
        out_shape=[jax.ShapeDtypeStruct((m, wd), dt) for wd, dt in zip(widths, dtypes)],
        compiler_params=pltpu.CompilerParams(
            dimension_semantics=("arbitrary",), vmem_limit_bytes=VMEM_LIMIT),
        name="inproj",
    )(x2d, g, w)


def _pack_pieces(x):
    packed = jnp.zeros_like(x)
    for part in range(3):
        piece = x.astype(jnp.bfloat16).astype(jnp.float32)
        x = x - piece
        packed = packed + (piece if part == 0 else pltpu.roll(piece, part * FOX_HEADS, axis=1))
    return packed.astype(jnp.bfloat16)


def _cumsum_kernel(f_ref, bias_ref, qx_ref, kx_ref):
    t_len = f_ref.shape[1]
    r = lax.broadcasted_iota(jnp.int32, (CUM_BLOCK, CUM_BLOCK), 0)
    c = lax.broadcasted_iota(jnp.int32, (CUM_BLOCK, CUM_BLOCK), 1)
    tri = (r >= c).astype(jnp.bfloat16)
    lane = lax.broadcasted_iota(jnp.int32, (CUM_BLOCK, LANES), 1)
    head_lane = lane < FOX_HEADS
    slot = lane % BIAS_SLOT
    used = lane < FOX_HEADS * BIAS_SLOT
    ones_q = (used & (slot >= 3) & (slot < 6)).astype(jnp.float32)
    ones_k = (used & (slot < 3)).astype(jnp.float32)
    pr = lax.broadcasted_iota(jnp.int32, (LANES, 2 * LANES), 0)
    pc = lax.broadcasted_iota(jnp.int32, (LANES, 2 * LANES), 1)
    part, head = pr // FOX_HEADS, pr % FOX_HEADS
    valid = pr < 3 * FOX_HEADS
    place = (jnp.where(valid & (pc == head * BIAS_SLOT + part), 1.0, 0.0)
             - jnp.where(valid & (pc == LANES + head * BIAS_SLOT + 3 + part), 1.0, 0.0)
             ).astype(jnp.bfloat16)
    carry = jnp.zeros((1, LANES), jnp.float32)
    for blk in range(t_len // CUM_BLOCK):
        rows = pl.ds(blk * CUM_BLOCK, CUM_BLOCK)
        v = f_ref[0, rows, :] + bias_ref[...]
        logf = jnp.where(head_lane, jnp.minimum(v, 0.0) - jnp.log1p(jnp.exp(-jnp.abs(v))), 0.0)
        sums = jnp.dot(tri, _pack_pieces(logf), preferred_element_type=jnp.float32)
        total = sums + pltpu.roll(sums, LANES - FOX_HEADS, axis=1) + pltpu.roll(
            sums, LANES - 2 * FOX_HEADS, axis=1)
        cs = jnp.where(head_lane, total, 0.0) + carry
        carry = cs[CUM_BLOCK - 1:CUM_BLOCK, :]
        spread = jnp.dot(_pack_pieces(cs * LOG2E), place, preferred_element_type=jnp.float32)
        qx_ref[0, rows, :] = (spread[:, 0:LANES] + ones_q).astype(jnp.bfloat16)
        kx_ref[0, rows, :] = (spread[:, LANES:] + ones_k).astype(jnp.bfloat16)


def _forget_cumsum(pf, bias_pad):
    b, t, _ = pf.shape
    spec = pl.BlockSpec((1, t, LANES), lambda i: (i, 0, 0))
    return pl.pallas_call(
        _cumsum_kernel,
        grid=(b,),
        in_specs=[spec, pl.BlockSpec((1, LANES), lambda i: (0, 0))],
        out_specs=[spec, spec],
        out_shape=[jax.ShapeDtypeStruct((b, t, LANES), jnp.bfloat16)] * 2,
        compiler_params=pltpu.CompilerParams(dimension_semantics=("arbitrary",)),
        name="forget_cumsum",
    )(pf, bias_pad)


def _hgrn_pool_kernel(lb_ref, gn_ref, pw_ref, ps_ref, proj_ref, o_ref, st_ref, *slots, layer):
    t_len = proj_ref.shape[1]
    w = HGRN_WIDTH
    c_len = HGRN_CHUNK
    half = c_len // 2
    rb = HGRN_BLOCK
    n_chunks = rb // c_len
    pairs = w // LANES

    raw = lb_ref[...]
    e = jnp.exp(raw - jnp.max(raw, axis=0, keepdims=True))
    p = e / jnp.sum(e, axis=0, keepdims=True)
    lb = jnp.sum(p[0:layer + 1, :], axis=0, keepdims=True) - p[0:1, :]

    ind = _head_indicator(w, jnp.bfloat16)
    r = lax.broadcasted_iota(jnp.int32, (rb, rb), 0)
    c = lax.broadcasted_iota(jnp.int32, (rb, rb), 1)
    tri = ((r >= c) & (r // c_len == c // c_len)).astype(jnp.bfloat16)
    row_tile = lax.broadcasted_iota(jnp.int32, (half, w), 0)
    first_head = lax.broadcasted_iota(jnp.int32, (HEAD_DIM, LANES), 1) < HEAD_DIM
    pool_row = lax.broadcasted_iota(jnp.int32, (rb, w), 0)
    pool_group = lax.broadcasted_iota(jnp.int32, (rb, w), 1) // HEAD_DIM
    gn = gn_ref[...]

    st_ref[...] = jnp.zeros_like(st_ref)
    n_blocks = t_len // rb
    ROW_B, ROW_K, ROW_V = range(3)

    def put_rows(rows_ref, which, x):
        for pr in range(pairs):
            rows_ref[which, pr] = x[:, pr * LANES:(pr + 1) * LANES]

    def row_tile_of(rows_ref, which, r):
        return jnp.concatenate([rows_ref[which, pr, pl.ds(r, half, stride=0), :]
                                for pr in range(pairs)], axis=1)

    def front(i, slot):
        a_ref, qd_ref, kd_ref, dl_ref, rows_ref = slot
        r0 = pl.multiple_of(i * rb, rb)
        rows = pl.ds(r0, rb)
        q = _silu(proj_ref[0, rows, 0:w])
        z = proj_ref[0, rows, w:2 * w]

        sig, sig_neg = _sigmoid_pair(z)
        f = lb + (1.0 - lb) * sig
        logf = jnp.log(jnp.maximum(f, TINY)) * LOG2E
        k = (1.0 - lb) * sig_neg
        b = jnp.zeros((rb, w), jnp.float32)
        rest = logf
        for _ in range(3):
            piece = rest.astype(jnp.bfloat16)
            rest = rest - piece.astype(jnp.float32)
            b = b + jnp.dot(tri, piece, preferred_element_type=jnp.float32)
        put_rows(rows_ref, ROW_B, b)
        put_rows(rows_ref, ROW_K, k)
        put_rows(rows_ref, ROW_V, proj_ref[0, rows, 2 * w:3 * w])
        qd_ref[...] = (q * jnp.exp2(b)).astype(jnp.bfloat16)
        kd_ref[...] = jnp.zeros_like(kd_ref)
        for ch in range(n_chunks):
            lo = ch * c_len
            b_last = row_tile_of(rows_ref, ROW_B, lo + c_len - 1)
            kd = jnp.concatenate(
                [k[t0:t0 + half, :] * jnp.exp2(b_last - b[t0:t0 + half, :]) for t0 in (lo, lo + half)],
                axis=0).astype(jnp.bfloat16)
            for pr in range(pairs):
                kd_ref[pr, lo:lo + c_len, ch * LANES:(ch + 1) * LANES] = kd[:, pr * LANES:(pr + 1) * LANES]
            dl_ref[ch:ch + 1, :] = jnp.exp2(b_last[0:1, :])

        parts = []
        for ch in range(n_chunks):
            lo = ch * c_len
            for s in range(c_len):
                bs = row_tile_of(rows_ref, ROW_B, lo + s)
                ks = row_tile_of(rows_ref, ROW_K, lo + s)
                for tl in range(s // half, c_len // half):
                    t0 = lo + tl * half
                    dec = jnp.exp2(b[t0:t0 + half, :] - bs)
                    if tl * half < s:
                        dec = jnp.where(row_tile + tl * half >= s, dec, 0.0)
                    parts.append(q[t0:t0 + half, :] * (ks * dec))
        n_half = len(parts) // 2
        for lo_p, hi_p in ((0, n_half), (n_half, len(parts))):
            a = jnp.concatenate(parts[lo_p:hi_p], axis=0).astype(jnp.bfloat16)
            a_ref[lo_p * half:hi_p * half, :] = jnp.dot(a, ind, preferred_element_type=jnp.float32)

    def back(i, slot):
        a_ref, qd_ref, kd_ref, dl_ref, rows_ref = slot
        r0 = pl.multiple_of(i * rb, rb)
        rows = pl.ds(r0, rb)
        gate = proj_ref[0, rows, 3 * w:4 * w]
        v_t = proj_ref[0, rows, 2 * w:3 * w].T.astype(jnp.bfloat16)
        upd_all = [jnp.dot(v_t[pr * LANES:(pr + 1) * LANES, :], kd_ref[pr],
                           preferred_element_type=jnp.float32)
                   for pr in range(pairs)]
        outs = []
        off = 0
        for ch in range(n_chunks):
            lo = ch * c_len
            o_tiles = [jnp.zeros((half, w), jnp.float32) for _ in range(c_len // half)]
            for s in range(c_len):
                vs = row_tile_of(rows_ref, ROW_V, lo + s)
                for tl in range(s // half, c_len // half):
                    o_tiles[tl] = o_tiles[tl] + a_ref[off:off + half, :] * vs
                    off += half

            dec_state = dl_ref[ch:ch + 1, :]
            inter = []
            for pr in range(pairs):
                lanes = slice(pr * LANES, (pr + 1) * LANES)
                st = st_ref[pr]
                wmat = jnp.concatenate([jnp.where(first_head, st, 0.0),
                                        jnp.where(first_head, 0.0, st)], axis=0).astype(jnp.bfloat16)
                inter.append(lax.dot_general(qd_ref[lo:lo + c_len, lanes], wmat,
                                             (((1,), (1,)), ((), ())),
                                             preferred_element_type=jnp.float32))
                upd = upd_all[pr][:, ch * LANES:(ch + 1) * LANES]
                st_ref[pr] = st * dec_state[:, lanes] + jnp.where(
                    first_head, upd[0:HEAD_DIM, :], upd[HEAD_DIM:, :])
            outs.append(jnp.concatenate(o_tiles, axis=0) + jnp.concatenate(inter, axis=1))
        o = jnp.concatenate(outs, axis=0)

        ms = jnp.dot((o * o).astype(jnp.bfloat16), ind,
                     preferred_element_type=jnp.float32) * (1.0 / HEAD_DIM)
        y = o * lax.rsqrt(ms + NORM_EPS) * gn
        o_ref[0, rows, 0:w] = (y * _silu(gate)).astype(o_ref.dtype)

        h0 = pl.multiple_of(jnp.maximum(r0 - POOL_HALO, 0), POOL_HALO)
        halo = proj_ref[0, pl.ds(h0, POOL_HALO), COL_POOL:COL_POOL + w]
        u = proj_ref[0, rows, COL_POOL:COL_POOL + w]
        x = jnp.concatenate([jnp.where(i > 0, halo, 0.0), u], axis=0)
        s2 = x + pltpu.roll(x, 1, axis=0)
        s4 = s2 + pltpu.roll(s2, 2, axis=0)
        s8 = s4 + pltpu.roll(s4, 4, axis=0)
        s16 = s8 + pltpu.roll(s8, 8, axis=0)
        tpos = (pool_row + r0 + 1).astype(jnp.float32)
        pooled = None
        for gi, (win, sm) in enumerate(zip(POOL_WINDOWS, (s2, s4, s8, s16))):
            val = sm[POOL_HALO:, :] / jnp.minimum(tpos, float(win))
            pooled = val if pooled is None else jnp.where(pool_group == gi, val, pooled)
        mixed = jnp.dot((pooled - u).astype(jnp.bfloat16), pw_ref[...],
                        preferred_element_type=jnp.float32)
        pool_gate = _silu(proj_ref[0, rows, COL_POOL + w:COL_POOL + 2 * w])
        o_ref[0, rows, w:2 * w] = ((mixed * ps_ref[...]) * pool_gate).astype(o_ref.dtype)

    slot_of = (slots[0:5], slots[5:10])
    front(0, slot_of[0])

    def several_blocks(j, carry):
        i0 = HGRN_UNROLL * j
        for u in range(HGRN_UNROLL):
            front(jnp.minimum(i0 + u + 1, n_blocks - 1), slot_of[(u + 1) % 2])
            back(i0 + u, slot_of[u % 2])
        return carry

    lax.fori_loop(0, n_blocks // HGRN_UNROLL, several_blocks, 0)


def _hgrn_pool(pa, lower_bounds, gn, pool_w_blockdiag, pool_scale, layer):
    b, t, width = pa.shape
    out_width = HGRN_WIDTH + POOL_WIDTH
    return pl.pallas_call(
        functools.partial(_hgrn_pool_kernel, layer=layer),
        grid=(b,),
        in_specs=[
            pl.BlockSpec((DEPTH, HGRN_WIDTH), lambda i: (0, 0)),
            pl.BlockSpec((1, HGRN_WIDTH), lambda i: (0, 0)),
            pl.BlockSpec((POOL_WIDTH, POOL_WIDTH), lambda i: (0, 0)),
            pl.BlockSpec((1, POOL_WIDTH), lambda i: (0, 0)),
            pl.BlockSpec((1, t, width), lambda i: (i, 0, 0)),
        ],
        out_specs=pl.BlockSpec((1, t, out_width), lambda i: (i, 0, 0)),
        out_shape=jax.ShapeDtypeStruct((b, t, out_width), jnp.bfloat16),
        scratch_shapes=[pltpu.VMEM((HGRN_WIDTH // LANES, HEAD_DIM, LANES), jnp.float32)] + 2 * [
            pltpu.VMEM((HGRN_BLOCK * 3 * HGRN_CHUNK // 4, HGRN_WIDTH), jnp.float32),
            pltpu.VMEM((HGRN_BLOCK, HGRN_WIDTH), jnp.bfloat16),
            pltpu.VMEM((HGRN_WIDTH // LANES, HGRN_BLOCK, HGRN_BLOCK // HGRN_CHUNK * LANES),
                       jnp.bfloat16),
            pltpu.VMEM((max(SUBLANES, HGRN_BLOCK // HGRN_CHUNK), HGRN_WIDTH), jnp.float32),
            pltpu.VMEM((3, HGRN_WIDTH // LANES, HGRN_BLOCK, LANES), jnp.float32),
        ],
        compiler_params=pltpu.CompilerParams(
            dimension_semantics=("arbitrary",), vmem_limit_bytes=VMEM_LIMIT),
        name="hgrn2_pool",
    )(lower_bounds, gn, pool_w_blockdiag, pool_scale, pa)


def _fox_kernel(q_ref, k_ref, v_ref, g_ref, qx_ref, kx_ref, o_ref, kcat_ref, qcat_ref, vaug_ref,
                acc_ref, st_ref):
    pair = pl.program_id(1)
    t_len = k_ref.shape[1]
    bq = ATT_BLOCK
    tile = ATT_DIAG_TILE
    n_qblocks = t_len // bq

    kcat_ref[:, 0:LANES] = k_ref[0]
    kcat_ref[:, LANES:2 * LANES] = kx_ref[0]
    vt = v_ref[0].astype(jnp.float32).T.astype(jnp.bfloat16)
    for h2 in range(2):
        vaug_ref[h2, 0:HEAD_DIM, :] = vt[h2 * HEAD_DIM:(h2 + 1) * HEAD_DIM, :]
        vaug_ref[h2, HEAD_DIM:, :] = jnp.ones((ATT_ONES_ROWS, t_len), jnp.bfloat16)
    lane = lax.broadcasted_iota(jnp.int32, (t_len, LANES), 1)
    qf = q_ref[0].astype(jnp.float32) * (HEAD_DIM ** -0.5 * LOG2E)
    qx = qx_ref[0]
    for h2 in range(2):
        qcat_ref[h2, :, 0:LANES] = jnp.where((lane // HEAD_DIM) == h2, qf, 0.0).astype(jnp.bfloat16)
        qcat_ref[h2, :, LANES:2 * LANES] = jnp.where(
            (lane // BIAS_SLOT) == pair * 2 + h2, qx, jnp.zeros_like(qx))
    acc_ref[...] = jnp.zeros_like(acc_ref)

    units = []
    for qb in range(n_qblocks):
        for kb in range(qb):
            units += [(qb, h2, qb * bq, bq, kb * bq, bq, False) for h2 in range(2)]
        for h2 in range(2):
            units += [(qb, h2, qb * bq + c * tile, tile, qb * bq, (c + 1) * tile, True)
                      for c in range(bq // tile)]

    def scores(slot, unit):
        _, h2, q0, nq, k0, nk, _ = unit
        st_ref[slot, 0:nk, 0:nq] = lax.dot_general(
            kcat_ref[k0:k0 + nk, :], qcat_ref[h2, q0:q0 + nq, :], (((1,), (1,)), ((), ())),
            preferred_element_type=jnp.float32)

    m_run = {(qb, h2, c): jnp.full((1, tile), MASK_VALUE, jnp.float32)
             for qb in range(n_qblocks) for h2 in range(2) for c in range(bq // tile)}

    def consume(slot, unit):
        qb, h2, q0, nq, k0, nk, masked = unit
        st = st_ref[slot, 0:nk, 0:nq]
        if masked:
            key = lax.broadcasted_iota(jnp.int32, (nk, nq), 0) + k0
            query = lax.broadcasted_iota(jnp.int32, (nk, nq), 1) + q0
            st = jnp.where(query >= key, st, MASK_VALUE)
        tiles = [(q0 - qb * bq) // tile + i for i in range(nq // tile)]
        m_prev = jnp.concatenate([m_run[(qb, h2, c)] for c in tiles], axis=1)
        m_new = jnp.maximum(m_prev, jnp.max(st, axis=0, keepdims=True))
        pt = jnp.exp2(st - m_new).astype(jnp.bfloat16)
        alpha = jnp.exp2(m_prev - m_new)
        cols = slice(q0 - qb * bq, q0 - qb * bq + nq)
        acc_ref[qb, h2, :, cols] = acc_ref[qb, h2, :, cols] * alpha + jnp.dot(
            vaug_ref[h2, :, k0:k0 + nk], pt, preferred_element_type=jnp.float32)
        for i, c in enumerate(tiles):
            m_run[(qb, h2, c)] = m_new[:, i * tile:(i + 1) * tile]

    def finish(qb):
        ot = jnp.concatenate(
            [acc_ref[qb, h2, 0:HEAD_DIM, :] / acc_ref[qb, h2, HEAD_DIM:HEAD_DIM + 1, :]
             for h2 in range(2)], axis=0)
        rows = slice(qb * bq, (qb + 1) * bq)
        gate = _silu(g_ref[0, rows, :].astype(jnp.float32))
        o_ref[0, rows, :] = (ot.T * gate).astype(o_ref.dtype)

    ahead = ATT_SLOTS - 1
    for u in range(min(ahead, len(units))):
        scores(u % ATT_SLOTS, units[u])
    for u, unit in enumerate(units):
        if u + ahead < len(units):
            scores((u + ahead) % ATT_SLOTS, units[u + ahead])
        consume(u % ATT_SLOTS, unit)
        if u + 1 == len(units) or units[u + 1][0] != unit[0]:
            finish(unit[0])


def _fox(pc, qx, kx):
    b, t, _ = pc.shape
    pairs = FOX_HEADS // 2
    blocks_per_section = FOX_WIDTH // LANES
    section = lambda sec: pl.BlockSpec(
        (1, t, LANES), lambda i, p: (i, 0, sec * blocks_per_section + p))
    shared = pl.BlockSpec((1, t, LANES), lambda i, p: (i, 0, 0))
    return pl.pallas_call(
        _fox_kernel,
        grid=(b, pairs),
        in_specs=[section(0), section(1), section(2), section(3), shared, shared],
        out_specs=pl.BlockSpec((1, t, LANES), lambda i, p: (i, 0, p)),
        out_shape=jax.ShapeDtypeStruct((b, t, FOX_WIDTH), jnp.bfloat16),
        scratch_shapes=[
            pltpu.VMEM((t, 2 * LANES), jnp.bfloat16),
            pltpu.VMEM((2, t, 2 * LANES), jnp.bfloat16),
            pltpu.VMEM((2, HEAD_DIM + ATT_ONES_ROWS, t), jnp.bfloat16),
            pltpu.VMEM((t // ATT_BLOCK, 2, HEAD_DIM + ATT_ONES_ROWS, ATT_BLOCK), jnp.float32),
            pltpu.VMEM((ATT_SLOTS, ATT_BLOCK, ATT_BLOCK), jnp.float32),
        ],
        compiler_params=pltpu.CompilerParams(
            dimension_semantics=("arbitrary", "arbitrary"), vmem_limit_bytes=VMEM_LIMIT),
        name="fox_attention",
    )(pc, pc, pc, pc, qx, kx)


def _outproj_kernel(x_ref, ab_ref, c_ref, w_ref, g_ref, o_ref):
    mixed = jnp.concatenate([ab_ref[...], c_ref[...]], axis=-1)
    y = jnp.dot(mixed, w_ref[...], preferred_element_type=jnp.float32)
    ms = jnp.mean(y * y, axis=-1, keepdims=True)
    o_ref[...] = x_ref[...] + y * lax.rsqrt(ms + NORM_EPS) * g_ref[...]


def _outproj(x2d, o_ab, o_c, w, g):
    m = x2d.shape[0]
    row_spec = lambda width: pl.BlockSpec((PROJ_ROWS, width), lambda i: (i, 0))
    return pl.pallas_call(
        _outproj_kernel,
        grid=(m // PROJ_ROWS,),
        in_specs=[
            row_spec(D_MODEL), row_spec(HGRN_WIDTH + POOL_WIDTH), row_spec(FOX_WIDTH),
            pl.BlockSpec((D_MODEL, D_MODEL), lambda i: (0, 0)),
            pl.BlockSpec((1, D_MODEL), lambda i: (0, 0)),
        ],
        out_specs=row_spec(D_MODEL),
        out_shape=jax.ShapeDtypeStruct((m, D_MODEL), jnp.float32),
        compiler_params=pltpu.CompilerParams(
            dimension_semantics=("arbitrary",), vmem_limit_bytes=VMEM_LIMIT),
        name="outproj",
    )(x2d, o_ab, o_c, w, g)


def kernel(x, lower_bounds, pre_norm_g, w_in, hgrn_norm_g, fox_f_bias, pool_w, pool_scale, w_out,
           post_norm_g):
    b, t, d = x.shape
    m = b * t
    x2d = x.reshape(m, d)
    for layer in range(DEPTH):
        w_in_l = jnp.pad(w_in[layer], ((0, 0), (0, IN_WIDTH_PAD - IN_WIDTH))).astype(jnp.bfloat16)
        bias_pad = jnp.pad(fox_f_bias[layer], (0, LANES - FOX_HEADS)).reshape(1, LANES)
        pool_bd = jax.scipy.linalg.block_diag(*pool_w[layer]).astype(jnp.bfloat16)

        pa, pc, pf = _inproj(x2d, pre_norm_g[layer].reshape(1, d), w_in_l)
        pa = pa.reshape(b, t, COL_Q)
        pc = pc.reshape(b, t, COL_FC - COL_Q)
        qx, kx = _forget_cumsum(pf.reshape(b, t, LANES), bias_pad)
        o_ab = _hgrn_pool(pa, lower_bounds, hgrn_norm_g[layer].reshape(1, HGRN_WIDTH), pool_bd,
                          pool_scale[layer].reshape(1, POOL_WIDTH), layer)
        o_c = _fox(pc, qx, kx)
        x2d = _outproj(x2d, o_ab.reshape(m, HGRN_WIDTH + POOL_WIDTH), o_c.reshape(m, FOX_WIDTH),
                       w_out[layer].astype(jnp.bfloat16), post_norm_g[layer].reshape(1, d))
    return x2d.reshape(b, t, d)
```

```python
import functools
import math

import jax
import jax.numpy as jnp
from jax import lax
from jax.experimental import pallas as pl
from jax.experimental.pallas import tpu as pltpu

D_MODEL = 1024
DEPTH = 2
NORM_EPS = 1e-6
MASK_VALUE = -1e30
TINY = 1e-30

HGRN_HEADS = 4
HGRN_WIDTH = 256
HEAD_DIM = 64
POOL_WINDOWS = (2, 4, 8, 16)
POOL_WIDTH = 256
FOX_HEADS = 8
FOX_WIDTH = 512
IN_WIDTH = 3592

LANES = 128
SUBLANES = 8
IN_WIDTH_PAD = 29 * LANES
VMEM_LIMIT = 56 * 1024 * 1024

COL_HGRN = 0
COL_POOL = 1024
COL_Q = 1536
COL_FC = 3584

HGRN_CHUNK = 16
HGRN_BLOCK = 128
HGRN_UNROLL = 8
PROJ_ROWS = 512
OUT_ROWS = 1024
ATT_BLOCK = 512
ATT_KEY_TILE = 512
ATT_DIAG_TILE = 256
ATT_SLOTS = 4
ATT_ONES_ROWS = 16
CUM_BLOCK = 256
POOL_HALO = 16

LOG2E = math.log2(math.e)
BIAS_SLOT = 8


def _sigmoid_pair(z):
    e = jnp.exp(-jnp.abs(z))
    r = 1.0 / (1.0 + e)
    big, small = r, e * r
    pos = z >= 0
    return jnp.where(pos, big, small), jnp.where(pos, small, big)


def _silu(x):
    return x * (1.0 / (1.0 + jnp.exp(-x)))


def _head_indicator(n, dtype):
    r = lax.broadcasted_iota(jnp.int32, (n, n), 0) // HEAD_DIM
    c = lax.broadcasted_iota(jnp.int32, (n, n), 1) // HEAD_DIM
    return (r == c).astype(dtype)


PROJ_WIDTHS = (COL_Q, COL_FC - COL_Q, IN_WIDTH_PAD - COL_FC)
PROJ_DTYPES = (jnp.float32, jnp.bfloat16, jnp.float32)


def _inproj_kernel(x_ref, g_ref, w_ref, wf_ref, pa_ref, pc_ref, pf_ref):
    x = x_ref[...]
    ms = jnp.mean(x * x, axis=-1, keepdims=True)
    h = (x * lax.rsqrt(ms + NORM_EPS) * g_ref[...]).astype(jnp.bfloat16)
    pa_ref[...] = jnp.dot(h, w_ref[0, :, 0:COL_Q].astype(jnp.bfloat16),
                          preferred_element_type=jnp.float32)
    pc_ref[...] = jnp.dot(h, w_ref[0, :, COL_Q:COL_FC].astype(jnp.bfloat16),
                          preferred_element_type=jnp.float32).astype(jnp.bfloat16)
    pf_ref[...] = jnp.dot(h, wf_ref[...], preferred_element_type=jnp.float32)


def _row_spec(width, rows=PROJ_ROWS):
    return pl.BlockSpec((rows, width), lambda i: (i, 0))


def _whole_spec(shape):
    return pl.BlockSpec(shape, lambda i: (0,) * len(shape))


def _inproj(x2d, g, w_all, w_forget, layer):
    m = x2d.shape[0]
    return pl.pallas_call(
        _inproj_kernel,
        grid=(m // PROJ_ROWS,),
        in_specs=[
            _row_spec(D_MODEL), _whole_spec((1, D_MODEL)),
            pl.BlockSpec((1, D_MODEL, IN_WIDTH), lambda i: (layer, 0, 0),
                         pipeline_mode=pl.Buffered(1)),
            _whole_spec((D_MODEL, LANES)),
        ],
        out_specs=[_row_spec(wd) for wd in PROJ_WIDTHS],
        out_shape=[jax.ShapeDtypeStruct((m, wd), dt) for wd, dt in zip(PROJ_WIDTHS, PROJ_DTYPES)],
        compiler_params=pltpu.CompilerParams(
            dimension_semantics=("arbitrary",), vmem_limit_bytes=VMEM_LIMIT),
        name="inproj",
    )(x2d, g, w_all, w_forget)


def _pack_pieces(x):
    packed = jnp.zeros_like(x)
    for part in range(3):
        piece = x.astype(jnp.bfloat16).astype(jnp.float32)
        x = x - piece
        packed = packed + (piece if part == 0 else pltpu.roll(piece, part * FOX_HEADS, axis=1))
    return packed.astype(jnp.bfloat16)


def _cumsum_kernel(f_ref, bias_ref, qx_ref, kx_ref):
    t_len = f_ref.shape[1]
    r = lax.broadcasted_iota(jnp.int32, (CUM_BLOCK, CUM_BLOCK), 0)
    c = lax.broadcasted_iota(jnp.int32, (CUM_BLOCK, CUM_BLOCK), 1)
    tri = (r >= c).astype(jnp.bfloat16)
    lane = lax.broadcasted_iota(jnp.int32, (CUM_BLOCK, LANES), 1)
    head_lane = lane < FOX_HEADS
    slot = lane % BIAS_SLOT
    used = lane < FOX_HEADS * BIAS_SLOT
    ones_q = (used & (slot >= 3) & (slot < 6)).astype(jnp.float32)
    ones_k = (used & (slot < 3)).astype(jnp.float32)
    pr = lax.broadcasted_iota(jnp.int32, (LANES, 2 * LANES), 0)
    pc = lax.broadcasted_iota(jnp.int32, (LANES, 2 * LANES), 1)
    part, head = pr // FOX_HEADS, pr % FOX_HEADS
    valid = pr < 3 * FOX_HEADS
    place = (jnp.where(valid & (pc == head * BIAS_SLOT + part), 1.0, 0.0)
             - jnp.where(valid & (pc == LANES + head * BIAS_SLOT + 3 + part), 1.0, 0.0)
             ).astype(jnp.bfloat16)
    carry = jnp.zeros((1, LANES), jnp.float32)
    for blk in range(t_len // CUM_BLOCK):
        rows = pl.ds(blk * CUM_BLOCK, CUM_BLOCK)
        v = f_ref[0, rows, :] + bias_ref[...]
        logf = jnp.where(head_lane, jnp.minimum(v, 0.0) - jnp.log1p(jnp.exp(-jnp.abs(v))), 0.0)
        sums = jnp.dot(tri, _pack_pieces(logf), preferred_element_type=jnp.float32)
        total = sums + pltpu.roll(sums, LANES - FOX_HEADS, axis=1) + pltpu.roll(
            sums, LANES - 2 * FOX_HEADS, axis=1)
        cs = jnp.where(head_lane, total, 0.0) + carry
        carry = cs[CUM_BLOCK - 1:CUM_BLOCK, :]
        spread = jnp.dot(_pack_pieces(cs * LOG2E), place, preferred_element_type=jnp.float32)
        qx_ref[0, rows, :] = (spread[:, 0:LANES] + ones_q).astype(jnp.bfloat16)
        kx_ref[0, rows, :] = (spread[:, LANES:] + ones_k).astype(jnp.bfloat16)


def _forget_cumsum(pf, bias_pad):
    b, t, _ = pf.shape
    spec = pl.BlockSpec((1, t, LANES), lambda i: (i, 0, 0))
    return pl.pallas_call(
        _cumsum_kernel,
        grid=(b,),
        in_specs=[spec, pl.BlockSpec((1, LANES), lambda i: (0, 0))],
        out_specs=[spec, spec],
        out_shape=[jax.ShapeDtypeStruct((b, t, LANES), jnp.bfloat16)] * 2,
        compiler_params=pltpu.CompilerParams(dimension_semantics=("arbitrary",)),
        name="forget_cumsum",
    )(pf, bias_pad)


def _hgrn_pool_kernel(lb_ref, gn_ref, pw_ref, ps_ref, proj_ref, o_ref, st_ref, *slots, layer):
    t_len = proj_ref.shape[1]
    w = HGRN_WIDTH
    c_len = HGRN_CHUNK
    half = c_len // 2
    rb = HGRN_BLOCK
    n_chunks = rb // c_len
    pairs = w // LANES

    raw = lb_ref[...]
    e = jnp.exp(raw - jnp.max(raw, axis=0, keepdims=True))
    p = e / jnp.sum(e, axis=0, keepdims=True)
    lb = jnp.sum(p[0:layer + 1, :], axis=0, keepdims=True) - p[0:1, :]

    ind = _head_indicator(w, jnp.bfloat16)
    r = lax.broadcasted_iota(jnp.int32, (rb, rb), 0)
    c = lax.broadcasted_iota(jnp.int32, (rb, rb), 1)
    tri = ((r >= c) & (r // c_len == c // c_len)).astype(jnp.bfloat16)
    row_tile = lax.broadcasted_iota(jnp.int32, (half, w), 0)
    first_head = lax.broadcasted_iota(jnp.int32, (HEAD_DIM, LANES), 1) < HEAD_DIM
    pool_row = lax.broadcasted_iota(jnp.int32, (rb, w), 0)
    pool_group = lax.broadcasted_iota(jnp.int32, (rb, w), 1) // HEAD_DIM
    gn = gn_ref[...]

    st_ref[...] = jnp.zeros_like(st_ref)
    n_blocks = t_len // rb
    ROW_B, ROW_K, ROW_V = range(3)

    def put_rows(rows_ref, which, x):
        for pr in range(pairs):
            rows_ref[which, pr] = x[:, pr * LANES:(pr + 1) * LANES]

    def row_tile_of(rows_ref, which, r):
        return jnp.concatenate([rows_ref[which, pr, pl.ds(r, half, stride=0), :]
                                for pr in range(pairs)], axis=1)

    def front(i, slot):
        a_ref, qd_ref, kd_ref, dl_ref, rows_ref = slot
        r0 = pl.multiple_of(i * rb, rb)
        rows = pl.ds(r0, rb)
        q = _silu(proj_ref[0, rows, 0:w])
        z = proj_ref[0, rows, w:2 * w]

        sig, sig_neg = _sigmoid_pair(z)
        f = lb + (1.0 - lb) * sig
        logf = jnp.log(jnp.maximum(f, TINY)) * LOG2E
        k = (1.0 - lb) * sig_neg
        b = jnp.zeros((rb, w), jnp.float32)
        rest = logf
        for _ in range(3):
            piece = rest.astype(jnp.bfloat16)
            rest = rest - piece.astype(jnp.float32)
            b = b + jnp.dot(tri, piece, preferred_element_type=jnp.float32)
        put_rows(rows_ref, ROW_B, b)
        put_rows(rows_ref, ROW_K, k)
        put_rows(rows_ref, ROW_V, proj_ref[0, rows, 2 * w:3 * w])
        qd_ref[...] = (q * jnp.exp2(b)).astype(jnp.bfloat16)
        kd_ref[...] = jnp.zeros_like(kd_ref)
        for ch in range(n_chunks):
            lo = ch * c_len
            b_last = row_tile_of(rows_ref, ROW_B, lo + c_len - 1)
            kd = jnp.concatenate(
                [k[t0:t0 + half, :] * jnp.exp2(b_last - b[t0:t0 + half, :]) for t0 in (lo, lo + half)],
                axis=0).astype(jnp.bfloat16)
            for pr in range(pairs):
                kd_ref[pr, lo:lo + c_len, ch * LANES:(ch + 1) * LANES] = kd[:, pr * LANES:(pr + 1) * LANES]
            dl_ref[ch:ch + 1, :] = jnp.exp2(b_last[0:1, :])

        parts = []
        for ch in range(n_chunks):
            lo = ch * c_len
            for s in range(c_len):
                bs = row_tile_of(rows_ref, ROW_B, lo + s)
                ks = row_tile_of(rows_ref, ROW_K, lo + s)
                for tl in range(s // half, c_len // half):
                    t0 = lo + tl * half
                    dec = jnp.exp2(b[t0:t0 + half, :] - bs)
                    if tl * half < s:
                        dec = jnp.where(row_tile + tl * half >= s, dec, 0.0)
                    parts.append(q[t0:t0 + half, :] * (ks * dec))
        n_half = len(parts) // 2
        for lo_p, hi_p in ((0, n_half), (n_half, len(parts))):
            a = jnp.concatenate(parts[lo_p:hi_p], axis=0).astype(jnp.bfloat16)
            a_ref[lo_p * half:hi_p * half, :] = jnp.dot(a, ind, preferred_element_type=jnp.float32)

    def back(i, slot):
        a_ref, qd_ref, kd_ref, dl_ref, rows_ref = slot
        r0 = pl.multiple_of(i * rb, rb)
        rows = pl.ds(r0, rb)
        gate = proj_ref[0, rows, 3 * w:4 * w]
        v_t = proj_ref[0, rows, 2 * w:3 * w].T.astype(jnp.bfloat16)
        upd_all = [jnp.dot(v_t[pr * LANES:(pr + 1) * LANES, :], kd_ref[pr],
                           preferred_element_type=jnp.float32)
                   for pr in range(pairs)]
        outs = []
        off = 0
        for ch in range(n_chunks):
            lo = ch * c_len
            o_tiles = [jnp.zeros((half, w), jnp.float32) for _ in range(c_len // half)]
            for s in range(c_len):
                vs = row_tile_of(rows_ref, ROW_V, lo + s)
                for tl in range(s // half, c_len // half):
                    o_tiles[tl] = o_tiles[tl] + a_ref[off:off + half, :] * vs
                    off += half

            dec_state = dl_ref[ch:ch + 1, :]
            inter = []
            for pr in range(pairs):
                lanes = slice(pr * LANES, (pr + 1) * LANES)
                st = st_ref[pr]
                wmat = jnp.concatenate([jnp.where(first_head, st, 0.0),
                                        jnp.where(first_head, 0.0, st)], axis=0).astype(jnp.bfloat16)
                inter.append(lax.dot_general(qd_ref[lo:lo + c_len, lanes], wmat,
                                             (((1,), (1,)), ((), ())),
                                             preferred_element_type=jnp.float32))
                upd = upd_all[pr][:, ch * LANES:(ch + 1) * LANES]
                st_ref[pr] = st * dec_state[:, lanes] + jnp.where(
                    first_head, upd[0:HEAD_DIM, :], upd[HEAD_DIM:, :])
            outs.append(jnp.concatenate(o_tiles, axis=0) + jnp.concatenate(inter, axis=1))
        o = jnp.concatenate(outs, axis=0)

        ms = jnp.dot((o * o).astype(jnp.bfloat16), ind,
                     preferred_element_type=jnp.float32) * (1.0 / HEAD_DIM)
        y = o * lax.rsqrt(ms + NORM_EPS) * gn
        o_ref[0, rows, 0:w] = (y * _silu(gate)).astype(o_ref.dtype)

        h0 = pl.multiple_of(jnp.maximum(r0 - POOL_HALO, 0), POOL_HALO)
        halo = proj_ref[0, pl.ds(h0, POOL_HALO), COL_POOL:COL_POOL + w]
        u = proj_ref[0, rows, COL_POOL:COL_POOL + w]
        x = jnp.concatenate([jnp.where(i > 0, halo, 0.0), u], axis=0)
        s2 = x + pltpu.roll(x, 1, axis=0)
        s4 = s2 + pltpu.roll(s2, 2, axis=0)
        s8 = s4 + pltpu.roll(s4, 4, axis=0)
        s16 = s8 + pltpu.roll(s8, 8, axis=0)
        tpos = (pool_row + r0 + 1).astype(jnp.float32)
        pooled = None
        for gi, (win, sm) in enumerate(zip(POOL_WINDOWS, (s2, s4, s8, s16))):
            val = sm[POOL_HALO:, :] / jnp.minimum(tpos, float(win))
            pooled = val if pooled is None else jnp.where(pool_group == gi, val, pooled)
        mixed = jnp.dot((pooled - u).astype(jnp.bfloat16), pw_ref[...],
                        preferred_element_type=jnp.float32)
        pool_gate = _silu(proj_ref[0, rows, COL_POOL + w:COL_POOL + 2 * w])
        o_ref[0, rows, w:2 * w] = ((mixed * ps_ref[...]) * pool_gate).astype(o_ref.dtype)

    slot_of = (slots[0:5], slots[5:10])
    front(0, slot_of[0])

    def several_blocks(j, carry):
        i0 = HGRN_UNROLL * j
        for u in range(HGRN_UNROLL):
            front(jnp.minimum(i0 + u + 1, n_blocks - 1), slot_of[(u + 1) % 2])
            back(i0 + u, slot_of[u % 2])
        return carry

    lax.fori_loop(0, n_blocks // HGRN_UNROLL, several_blocks, 0)


def _hgrn_pool(pa, lower_bounds, gn, pool_w_blockdiag, pool_scale, layer):
    b, t, width = pa.shape
    out_width = HGRN_WIDTH + POOL_WIDTH
    return pl.pallas_call(
        functools.partial(_hgrn_pool_kernel, layer=layer),
        grid=(b,),
        in_specs=[
            pl.BlockSpec((DEPTH, HGRN_WIDTH), lambda i: (0, 0)),
            pl.BlockSpec((1, HGRN_WIDTH), lambda i: (0, 0)),
            pl.BlockSpec((POOL_WIDTH, POOL_WIDTH), lambda i: (0, 0)),
            pl.BlockSpec((1, POOL_WIDTH), lambda i: (0, 0)),
            pl.BlockSpec((1, t, width), lambda i: (i, 0, 0)),
        ],
        out_specs=pl.BlockSpec((1, t, out_width), lambda i: (i, 0, 0)),
        out_shape=jax.ShapeDtypeStruct((b, t, out_width), jnp.bfloat16),
        scratch_shapes=[pltpu.VMEM((HGRN_WIDTH // LANES, HEAD_DIM, LANES), jnp.float32)] + 2 * [
            pltpu.VMEM((HGRN_BLOCK * 3 * HGRN_CHUNK // 4, HGRN_WIDTH), jnp.float32),
            pltpu.VMEM((HGRN_BLOCK, HGRN_WIDTH), jnp.bfloat16),
            pltpu.VMEM((HGRN_WIDTH // LANES, HGRN_BLOCK, HGRN_BLOCK // HGRN_CHUNK * LANES),
                       jnp.bfloat16),
            pltpu.VMEM((max(SUBLANES, HGRN_BLOCK // HGRN_CHUNK), HGRN_WIDTH), jnp.float32),
            pltpu.VMEM((3, HGRN_WIDTH // LANES, HGRN_BLOCK, LANES), jnp.float32),
        ],
        compiler_params=pltpu.CompilerParams(
            dimension_semantics=("arbitrary",), vmem_limit_bytes=VMEM_LIMIT),
        name="hgrn2_pool",
    )(lower_bounds, gn, pool_w_blockdiag, pool_scale, pa)


def _fox_kernel(q_ref, k_ref, v_ref, g_ref, qx_ref, kx_ref, o_ref, kcat_ref, qcat_ref, vaug_ref,
                acc_ref, st_ref):
    pair = pl.program_id(1)
    t_len = k_ref.shape[1]
    bq = ATT_BLOCK
    tile = ATT_DIAG_TILE
    n_qblocks = t_len // bq

    kcat_ref[:, 0:LANES] = k_ref[0]
    kcat_ref[:, LANES:2 * LANES] = kx_ref[0]
    vt = v_ref[0].astype(jnp.float32).T.astype(jnp.bfloat16)
    for h2 in range(2):
        vaug_ref[h2, 0:HEAD_DIM, :] = vt[h2 * HEAD_DIM:(h2 + 1) * HEAD_DIM, :]
        vaug_ref[h2, HEAD_DIM:, :] = jnp.ones((ATT_ONES_ROWS, t_len), jnp.bfloat16)
    lane = lax.broadcasted_iota(jnp.int32, (t_len, LANES), 1)
    qf = q_ref[0].astype(jnp.float32) * (HEAD_DIM ** -0.5 * LOG2E)
    qx = qx_ref[0]
    for h2 in range(2):
        qcat_ref[h2, :, 0:LANES] = jnp.where((lane // HEAD_DIM) == h2, qf, 0.0).astype(jnp.bfloat16)
        qcat_ref[h2, :, LANES:2 * LANES] = jnp.where(
            (lane // BIAS_SLOT) == pair * 2 + h2, qx, jnp.zeros_like(qx))
    acc_ref[...] = jnp.zeros_like(acc_ref)

    units = []
    for qb in range(n_qblocks):
        for k0 in range(0, qb * bq, ATT_KEY_TILE):
            units += [(qb, h2, qb * bq, bq, k0, min(ATT_KEY_TILE, qb * bq - k0), False)
                      for h2 in range(2)]
        for h2 in range(2):
            units += [(qb, h2, qb * bq + c * tile, tile, qb * bq, (c + 1) * tile, True)
                      for c in range(bq // tile)]

    def scores(slot, unit):
        _, h2, q0, nq, k0, nk, _ = unit
        st_ref[slot, 0:nk, 0:nq] = lax.dot_general(
            kcat_ref[k0:k0 + nk, :], qcat_ref[h2, q0:q0 + nq, :], (((1,), (1,)), ((), ())),
            preferred_element_type=jnp.float32)

    m_run = {(qb, h2, c): jnp.full((1, tile), MASK_VALUE, jnp.float32)
             for qb in range(n_qblocks) for h2 in range(2) for c in range(bq // tile)}

    def consume(slot, unit):
        qb, h2, q0, nq, k0, nk, masked = unit
        st = st_ref[slot, 0:nk, 0:nq]
        if masked:
            key = lax.broadcasted_iota(jnp.int32, (nk, nq), 0) + k0
            query = lax.broadcasted_iota(jnp.int32, (nk, nq), 1) + q0
            st = jnp.where(query >= key, st, MASK_VALUE)
        tiles = [(q0 - qb * bq) // tile + i for i in range(nq // tile)]
        m_prev = jnp.concatenate([m_run[(qb, h2, c)] for c in tiles], axis=1)
        m_new = jnp.maximum(m_prev, jnp.max(st, axis=0, keepdims=True))
        pt = jnp.exp2(st - m_new).astype(jnp.bfloat16)
        alpha = jnp.exp2(m_prev - m_new)
        cols = slice(q0 - qb * bq, q0 - qb * bq + nq)
        acc_ref[qb, h2, :, cols] = acc_ref[qb, h2, :, cols] * alpha + jnp.dot(
            vaug_ref[h2, :, k0:k0 + nk], pt, preferred_element_type=jnp.float32)
        for i, c in enumerate(tiles):
            m_run[(qb, h2, c)] = m_new[:, i * tile:(i + 1) * tile]

    def finish(qb):
        ot = jnp.concatenate(
            [acc_ref[qb, h2, 0:HEAD_DIM, :] / acc_ref[qb, h2, HEAD_DIM:HEAD_DIM + 1, :]
             for h2 in range(2)], axis=0)
        rows = slice(qb * bq, (qb + 1) * bq)
        gate = _silu(g_ref[0, rows, :].astype(jnp.float32))
        o_ref[0, rows, :] = (ot.T * gate).astype(o_ref.dtype)

    ahead = ATT_SLOTS - 1
    for u in range(min(ahead, len(units))):
        scores(u % ATT_SLOTS, units[u])
    for u, unit in enumerate(units):
        if u + ahead < len(units):
            scores((u + ahead) % ATT_SLOTS, units[u + ahead])
        consume(u % ATT_SLOTS, unit)
        if u + 1 == len(units) or units[u + 1][0] != unit[0]:
            finish(unit[0])


def _fox(pc, qx, kx):
    b, t, _ = pc.shape
    pairs = FOX_HEADS // 2
    blocks_per_section = FOX_WIDTH // LANES
    section = lambda sec: pl.BlockSpec(
        (1, t, LANES), lambda i, p: (i, 0, sec * blocks_per_section + p))
    shared = pl.BlockSpec((1, t, LANES), lambda i, p: (i, 0, 0))
    return pl.pallas_call(
        _fox_kernel,
        grid=(b, pairs),
        in_specs=[section(0), section(1), section(2), section(3), shared, shared],
        out_specs=pl.BlockSpec((1, t, LANES), lambda i, p: (i, 0, p)),
        out_shape=jax.ShapeDtypeStruct((b, t, FOX_WIDTH), jnp.bfloat16),
        scratch_shapes=[
            pltpu.VMEM((t, 2 * LANES), jnp.bfloat16),
            pltpu.VMEM((2, t, 2 * LANES), jnp.bfloat16),
            pltpu.VMEM((2, HEAD_DIM + ATT_ONES_ROWS, t), jnp.bfloat16),
            pltpu.VMEM((t // ATT_BLOCK, 2, HEAD_DIM + ATT_ONES_ROWS, ATT_BLOCK), jnp.float32),
            pltpu.VMEM((ATT_SLOTS, max(ATT_KEY_TILE, ATT_BLOCK), ATT_BLOCK), jnp.float32),
        ],
        compiler_params=pltpu.CompilerParams(
            dimension_semantics=("arbitrary", "arbitrary"), vmem_limit_bytes=VMEM_LIMIT),
        name="fox_attention",
    )(pc, pc, pc, pc, qx, kx)


def _project_out(x_ref, ab_ref, c_ref, w_ref, g_ref):
    mixed = jnp.concatenate([ab_ref[...], c_ref[...]], axis=-1)
    y = jnp.dot(mixed, w_ref[...], preferred_element_type=jnp.float32)
    ms = jnp.mean(y * y, axis=-1, keepdims=True)
    return x_ref[...] + y * lax.rsqrt(ms + NORM_EPS) * g_ref[...]


def _outproj_kernel(x_ref, ab_ref, c_ref, w_ref, g_ref, o_ref):
    o_ref[...] = _project_out(x_ref, ab_ref, c_ref, w_ref, g_ref)


def _outproj(x2d, o_ab, o_c, w, g):
    m = x2d.shape[0]
    return pl.pallas_call(
        _outproj_kernel,
        grid=(m // OUT_ROWS,),
        in_specs=[
            _row_spec(D_MODEL, OUT_ROWS), _row_spec(HGRN_WIDTH + POOL_WIDTH, OUT_ROWS),
            _row_spec(FOX_WIDTH, OUT_ROWS),
            _whole_spec((D_MODEL, D_MODEL)), _whole_spec((1, D_MODEL)),
        ],
        out_specs=_row_spec(D_MODEL, OUT_ROWS),
        out_shape=jax.ShapeDtypeStruct((m, D_MODEL), jnp.float32),
        compiler_params=pltpu.CompilerParams(
            dimension_semantics=("arbitrary",), vmem_limit_bytes=VMEM_LIMIT),
        name="outproj",
    )(x2d, o_ab, o_c, w, g)


def kernel(x, lower_bounds, pre_norm_g, w_in, hgrn_norm_g, fox_f_bias, pool_w, pool_scale, w_out,
           post_norm_g):
    b, t, d = x.shape
    m = b * t
    x2d = x.reshape(m, d)
    for layer in range(DEPTH):
        w_forget = jnp.pad(w_in[layer, :, COL_FC:], ((0, 0), (0, IN_WIDTH_PAD - IN_WIDTH))
                           ).astype(jnp.bfloat16)
        bias_pad = jnp.pad(fox_f_bias[layer], (0, LANES - FOX_HEADS)).reshape(1, LANES)
        pool_bd = jax.scipy.linalg.block_diag(*pool_w[layer]).astype(jnp.bfloat16)

        pa, pc, pf = _inproj(x2d, pre_norm_g[layer].reshape(1, d), w_in, w_forget, layer)
        qx, kx = _forget_cumsum(pf.reshape(b, t, LANES), bias_pad)
        o_ab = _hgrn_pool(pa.reshape(b, t, COL_Q), lower_bounds,
                          hgrn_norm_g[layer].reshape(1, HGRN_WIDTH), pool_bd,
                          pool_scale[layer].reshape(1, POOL_WIDTH), layer)
        o_c = _fox(pc.reshape(b, t, COL_FC - COL_Q), qx, kx)
        x2d = _outproj(x2d, o_ab.reshape(m, HGRN_WIDTH + POOL_WIDTH), o_c.reshape(m, FOX_WIDTH),
                       w_out[layer].astype(jnp.bfloat16), post_norm_g[layer].reshape(1, d))
    return x2d.reshape(b, t, d)
```

```python
import functools
import math

import jax
import jax.numpy as jnp
from jax import lax
from jax.experimental import pallas as pl
from jax.experimental.pallas import tpu as pltpu

D_MODEL = 1024
DEPTH = 2
NORM_EPS = 1e-6
MASK_VALUE = -1e30
TINY = 1e-30

HGRN_HEADS = 4
HGRN_WIDTH = 256
HEAD_DIM = 64
POOL_WINDOWS = (2, 4, 8, 16)
POOL_WIDTH = 256
FOX_HEADS = 8
FOX_WIDTH = 512
IN_WIDTH = 3592

LANES = 128
SUBLANES = 8
IN_WIDTH_PAD = 29 * LANES
VMEM_LIMIT = 56 * 1024 * 1024

COL_HGRN = 0
COL_POOL = 1024
COL_Q = 1536
COL_FC = 3584

HGRN_CHUNK = 16
HGRN_BLOCK = 256
HGRN_UNROLL = 4
PROJ_ROWS = 512
OUT_ROWS = 1024
ATT_BLOCK = 512
ATT_KEY_TILE = 512
ATT_DIAG_TILE = 256
ATT_PAIRS = 1
ATT_SLOTS = 4
ATT_ONES_ROWS = 16
CUM_BLOCK = 256
POOL_HALO = 16

LOG2E = math.log2(math.e)
BIAS_SLOT = 8


def _sigmoid_pair(z):
    e = jnp.exp(-jnp.abs(z))
    r = 1.0 / (1.0 + e)
    big, small = r, e * r
    pos = z >= 0
    return jnp.where(pos, big, small), jnp.where(pos, small, big)


def _silu(x):
    return x * (1.0 / (1.0 + jnp.exp2(x * (-LOG2E))))


def _head_indicator(n, dtype):
    r = lax.broadcasted_iota(jnp.int32, (n, n), 0) // HEAD_DIM
    c = lax.broadcasted_iota(jnp.int32, (n, n), 1) // HEAD_DIM
    return (r == c).astype(dtype)


PROJ_WIDTHS = (COL_Q, COL_FC - COL_Q, IN_WIDTH_PAD - COL_FC)
PROJ_DTYPES = (jnp.float32, jnp.bfloat16, jnp.float32)


def _inproj_kernel(x_ref, g_ref, wt_ref, pa_ref, pc_ref, pf_ref):
    x = x_ref[...]
    ms = jnp.mean(x * x, axis=-1, keepdims=True)
    h = (x * lax.rsqrt(ms + NORM_EPS) * g_ref[...]).astype(jnp.bfloat16)

    def project(w_rows):
        return lax.dot_general(h, w_rows.astype(jnp.bfloat16), (((1,), (1,)), ((), ())),
                               preferred_element_type=jnp.float32)

    pa_ref[...] = project(wt_ref[0, 0:COL_Q, :])
    pc_ref[...] = project(wt_ref[0, COL_Q:COL_FC, :]).astype(jnp.bfloat16)
    forget_rows = jnp.concatenate(
        [wt_ref[0, COL_FC:IN_WIDTH, :],
         jnp.zeros((IN_WIDTH_PAD - IN_WIDTH, D_MODEL), jnp.float32)], axis=0)
    pf_ref[...] = project(forget_rows)


def _row_spec(width, rows=PROJ_ROWS):
    return pl.BlockSpec((rows, width), lambda i: (i, 0))


def _whole_spec(shape):
    return pl.BlockSpec(shape, lambda i: (0,) * len(shape))


def _inproj(x2d, g, w_t, layer):
    m = x2d.shape[0]
    return pl.pallas_call(
        _inproj_kernel,
        grid=(m // PROJ_ROWS,),
        in_specs=[
            _row_spec(D_MODEL), _whole_spec((1, D_MODEL)),
            pl.BlockSpec((1, IN_WIDTH, D_MODEL), lambda i: (layer, 0, 0),
                         pipeline_mode=pl.Buffered(1)),
        ],
        out_specs=[_row_spec(wd) for wd in PROJ_WIDTHS],
        out_shape=[jax.ShapeDtypeStruct((m, wd), dt) for wd, dt in zip(PROJ_WIDTHS, PROJ_DTYPES)],
        compiler_params=pltpu.CompilerParams(
            dimension_semantics=("arbitrary",), vmem_limit_bytes=VMEM_LIMIT),
        name="inproj",
    )(x2d, g, w_t)


def _pack_pieces(x):
    packed = jnp.zeros_like(x)
    for part in range(3):
        piece = x.astype(jnp.bfloat16).astype(jnp.float32)
        x = x - piece
        packed = packed + (piece if part == 0 else pltpu.roll(piece, part * FOX_HEADS, axis=1))
    return packed.astype(jnp.bfloat16)


def _cumsum_kernel(f_ref, bias_ref, qx_ref, kx_ref):
    t_len = f_ref.shape[1]
    r = lax.broadcasted_iota(jnp.int32, (CUM_BLOCK, CUM_BLOCK), 0)
    c = lax.broadcasted_iota(jnp.int32, (CUM_BLOCK, CUM_BLOCK), 1)
    tri = (r >= c).astype(jnp.bfloat16)
    lane = lax.broadcasted_iota(jnp.int32, (CUM_BLOCK, LANES), 1)
    head_lane = lane < FOX_HEADS
    slot = lane % BIAS_SLOT
    used = lane < FOX_HEADS * BIAS_SLOT
    ones_q = (used & (slot >= 3) & (slot < 6)).astype(jnp.float32)
    ones_k = (used & (slot < 3)).astype(jnp.float32)
    pr = lax.broadcasted_iota(jnp.int32, (LANES, 2 * LANES), 0)
    pc = lax.broadcasted_iota(jnp.int32, (LANES, 2 * LANES), 1)
    part, head = pr // FOX_HEADS, pr % FOX_HEADS
    valid = pr < 3 * FOX_HEADS
    place = (jnp.where(valid & (pc == head * BIAS_SLOT + part), 1.0, 0.0)
             - jnp.where(valid & (pc == LANES + head * BIAS_SLOT + 3 + part), 1.0, 0.0)
             ).astype(jnp.bfloat16)
    carry = jnp.zeros((1, LANES), jnp.float32)
    for blk in range(t_len // CUM_BLOCK):
        rows = pl.ds(blk * CUM_BLOCK, CUM_BLOCK)
        v = f_ref[0, rows, :] + bias_ref[...]
        logf = jnp.where(head_lane, jnp.minimum(v, 0.0) - jnp.log1p(jnp.exp(-jnp.abs(v))), 0.0)
        sums = jnp.dot(tri, _pack_pieces(logf), preferred_element_type=jnp.float32)
        total = sums + pltpu.roll(sums, LANES - FOX_HEADS, axis=1) + pltpu.roll(
            sums, LANES - 2 * FOX_HEADS, axis=1)
        cs = jnp.where(head_lane, total, 0.0) + carry
        carry = cs[CUM_BLOCK - 1:CUM_BLOCK, :]
        spread = jnp.dot(_pack_pieces(cs * LOG2E), place, preferred_element_type=jnp.float32)
        qx_ref[0, rows, :] = (spread[:, 0:LANES] + ones_q).astype(jnp.bfloat16)
        kx_ref[0, rows, :] = (spread[:, LANES:] + ones_k).astype(jnp.bfloat16)


def _forget_cumsum(pf, bias_pad):
    b, t, _ = pf.shape
    spec = pl.BlockSpec((1, t, LANES), lambda i: (i, 0, 0))
    return pl.pallas_call(
        _cumsum_kernel,
        grid=(b,),
        in_specs=[spec, pl.BlockSpec((1, LANES), lambda i: (0, 0))],
        out_specs=[spec, spec],
        out_shape=[jax.ShapeDtypeStruct((b, t, LANES), jnp.bfloat16)] * 2,
        compiler_params=pltpu.CompilerParams(dimension_semantics=("arbitrary",)),
        name="forget_cumsum",
    )(pf, bias_pad)


def _hgrn_pool_kernel(lb_ref, gn_ref, pw_ref, ps_ref, proj_ref, o_ref, st_ref, *slots, layer):
    t_len = proj_ref.shape[1]
    w = HGRN_WIDTH
    c_len = HGRN_CHUNK
    half = c_len // 2
    rb = HGRN_BLOCK
    n_chunks = rb // c_len
    pairs = w // LANES

    raw = lb_ref[...]
    e = jnp.exp(raw - jnp.max(raw, axis=0, keepdims=True))
    p = e / jnp.sum(e, axis=0, keepdims=True)
    lb = jnp.sum(p[0:layer + 1, :], axis=0, keepdims=True) - p[0:1, :]

    ind = _head_indicator(w, jnp.bfloat16)
    r = lax.broadcasted_iota(jnp.int32, (rb, rb), 0)
    c = lax.broadcasted_iota(jnp.int32, (rb, rb), 1)
    tri = ((r >= c) & (r // c_len == c // c_len)).astype(jnp.bfloat16)
    row_tile = lax.broadcasted_iota(jnp.int32, (half, w), 0)
    first_head = lax.broadcasted_iota(jnp.int32, (HEAD_DIM, LANES), 1) < HEAD_DIM
    pool_row = lax.broadcasted_iota(jnp.int32, (rb, w), 0)
    pool_group = lax.broadcasted_iota(jnp.int32, (rb, w), 1) // HEAD_DIM
    gn = gn_ref[...]

    st_ref[...] = jnp.zeros_like(st_ref)
    n_blocks = t_len // rb
    ROW_B, ROW_K, ROW_V = range(3)

    def put_rows(rows_ref, which, x):
        for pr in range(pairs):
            rows_ref[which, pr] = x[:, pr * LANES:(pr + 1) * LANES]

    def row_tile_of(rows_ref, which, r):
        return jnp.concatenate([rows_ref[which, pr, pl.ds(r, half, stride=0), :]
                                for pr in range(pairs)], axis=1)

    def front(i, slot):
        a_ref, qd_ref, kd_ref, dl_ref, rows_ref = slot
        r0 = pl.multiple_of(i * rb, rb)
        rows = pl.ds(r0, rb)
        q = _silu(proj_ref[0, rows, 0:w])
        z = proj_ref[0, rows, w:2 * w]

        sig, sig_neg = _sigmoid_pair(z)
        f = lb + (1.0 - lb) * sig
        logf = jnp.log(jnp.maximum(f, TINY)) * LOG2E
        k = (1.0 - lb) * sig_neg
        b = jnp.zeros((rb, w), jnp.float32)
        rest = logf
        for _ in range(3):
            piece = rest.astype(jnp.bfloat16)
            rest = rest - piece.astype(jnp.float32)
            b = b + jnp.dot(tri, piece, preferred_element_type=jnp.float32)
        put_rows(rows_ref, ROW_B, b)
        put_rows(rows_ref, ROW_K, k)
        put_rows(rows_ref, ROW_V, proj_ref[0, rows, 2 * w:3 * w])
        qd_ref[...] = (q * jnp.exp2(b)).astype(jnp.bfloat16)
        kd_ref[...] = jnp.zeros_like(kd_ref)
        for ch in range(n_chunks):
            lo = ch * c_len
            b_last = row_tile_of(rows_ref, ROW_B, lo + c_len - 1)
            kd = jnp.concatenate(
                [k[t0:t0 + half, :] * jnp.exp2(b_last - b[t0:t0 + half, :]) for t0 in (lo, lo + half)],
                axis=0).astype(jnp.bfloat16)
            for pr in range(pairs):
                kd_ref[pr, lo:lo + c_len, ch * LANES:(ch + 1) * LANES] = kd[:, pr * LANES:(pr + 1) * LANES]
            dl_ref[ch:ch + 1, :] = jnp.exp2(b_last[0:1, :])

        parts = []
        for ch in range(n_chunks):
            lo = ch * c_len
            for s in range(c_len):
                bs = row_tile_of(rows_ref, ROW_B, lo + s)
                ks = row_tile_of(rows_ref, ROW_K, lo + s)
                for tl in range(s // half, c_len // half):
                    t0 = lo + tl * half
                    dec = jnp.exp2(b[t0:t0 + half, :] - bs)
                    if tl * half < s:
                        dec = jnp.where(row_tile + tl * half >= s, dec, 0.0)
                    parts.append(q[t0:t0 + half, :] * (ks * dec))
        n_half = len(parts) // 2
        for lo_p, hi_p in ((0, n_half), (n_half, len(parts))):
            a = jnp.concatenate(parts[lo_p:hi_p], axis=0).astype(jnp.bfloat16)
            a_ref[lo_p * half:hi_p * half, :] = jnp.dot(a, ind, preferred_element_type=jnp.float32)

    def back(i, slot):
        a_ref, qd_ref, kd_ref, dl_ref, rows_ref = slot
        r0 = pl.multiple_of(i * rb, rb)
        rows = pl.ds(r0, rb)
        gate = proj_ref[0, rows, 3 * w:4 * w]
        v_t = proj_ref[0, rows, 2 * w:3 * w].T.astype(jnp.bfloat16)
        upd_all = [jnp.dot(v_t[pr * LANES:(pr + 1) * LANES, :], kd_ref[pr],
                           preferred_element_type=jnp.float32)
                   for pr in range(pairs)]
        outs = []
        off = 0
        for ch in range(n_chunks):
            lo = ch * c_len
            o_tiles = [jnp.zeros((half, w), jnp.float32) for _ in range(c_len // half)]
            for s in range(c_len):
                vs = row_tile_of(rows_ref, ROW_V, lo + s)
                for tl in range(s // half, c_len // half):
                    o_tiles[tl] = o_tiles[tl] + a_ref[off:off + half, :] * vs
                    off += half

            dec_state = dl_ref[ch:ch + 1, :]
            inter = []
            for pr in range(pairs):
                lanes = slice(pr * LANES, (pr + 1) * LANES)
                st = st_ref[pr]
                wmat = jnp.concatenate([jnp.where(first_head, st, 0.0),
                                        jnp.where(first_head, 0.0, st)], axis=0).astype(jnp.bfloat16)
                inter.append(lax.dot_general(qd_ref[lo:lo + c_len, lanes], wmat,
                                             (((1,), (1,)), ((), ())),
                                             preferred_element_type=jnp.float32))
                upd = upd_all[pr][:, ch * LANES:(ch + 1) * LANES]
                st_ref[pr] = st * dec_state[:, lanes] + jnp.where(
                    first_head, upd[0:HEAD_DIM, :], upd[HEAD_DIM:, :])
            outs.append(jnp.concatenate(o_tiles, axis=0) + jnp.concatenate(inter, axis=1))
        o = jnp.concatenate(outs, axis=0)

        ms = jnp.dot((o * o).astype(jnp.bfloat16), ind,
                     preferred_element_type=jnp.float32) * (1.0 / HEAD_DIM)
        y = o * lax.rsqrt(ms + NORM_EPS) * gn
        o_ref[0, rows, 0:w] = (y * _silu(gate)).astype(o_ref.dtype)

        h0 = pl.multiple_of(jnp.maximum(r0 - POOL_HALO, 0), POOL_HALO)
        halo = proj_ref[0, pl.ds(h0, POOL_HALO), COL_POOL:COL_POOL + w]
        u = proj_ref[0, rows, COL_POOL:COL_POOL + w]
        x = jnp.concatenate([jnp.where(i > 0, halo, 0.0), u], axis=0)
        s2 = x + pltpu.roll(x, 1, axis=0)
        s4 = s2 + pltpu.roll(s2, 2, axis=0)
        s8 = s4 + pltpu.roll(s4, 4, axis=0)
        s16 = s8 + pltpu.roll(s8, 8, axis=0)
        tpos = (pool_row + r0 + 1).astype(jnp.float32)
        pooled = None
        for gi, (win, sm) in enumerate(zip(POOL_WINDOWS, (s2, s4, s8, s16))):
            val = sm[POOL_HALO:, :] / jnp.minimum(tpos, float(win))
            pooled = val if pooled is None else jnp.where(pool_group == gi, val, pooled)
        mixed = jnp.dot((pooled - u).astype(jnp.bfloat16), pw_ref[...],
                        preferred_element_type=jnp.float32)
        pool_gate = _silu(proj_ref[0, rows, COL_POOL + w:COL_POOL + 2 * w])
        o_ref[0, rows, w:2 * w] = ((mixed * ps_ref[...]) * pool_gate).astype(o_ref.dtype)

    slot_of = (slots[0:5], slots[5:10])
    front(0, slot_of[0])

    def several_blocks(j, carry):
        i0 = HGRN_UNROLL * j
        for u in range(HGRN_UNROLL):
            front(jnp.minimum(i0 + u + 1, n_blocks - 1), slot_of[(u + 1) % 2])
            back(i0 + u, slot_of[u % 2])
        return carry

    lax.fori_loop(0, n_blocks // HGRN_UNROLL, several_blocks, 0)


def _hgrn_pool(pa, lower_bounds, gn, pool_w_blockdiag, pool_scale, layer):
    b, t, width = pa.shape
    out_width = HGRN_WIDTH + POOL_WIDTH
    return pl.pallas_call(
        functools.partial(_hgrn_pool_kernel, layer=layer),
        grid=(b,),
        in_specs=[
            pl.BlockSpec((DEPTH, HGRN_WIDTH), lambda i: (0, 0)),
            pl.BlockSpec((1, HGRN_WIDTH), lambda i: (0, 0)),
            pl.BlockSpec((POOL_WIDTH, POOL_WIDTH), lambda i: (0, 0)),
            pl.BlockSpec((1, POOL_WIDTH), lambda i: (0, 0)),
            pl.BlockSpec((1, t, width), lambda i: (i, 0, 0)),
        ],
        out_specs=pl.BlockSpec((1, t, out_width), lambda i: (i, 0, 0)),
        out_shape=jax.ShapeDtypeStruct((b, t, out_width), jnp.bfloat16),
        scratch_shapes=[pltpu.VMEM((HGRN_WIDTH // LANES, HEAD_DIM, LANES), jnp.float32)] + 2 * [
            pltpu.VMEM((HGRN_BLOCK * 3 * HGRN_CHUNK // 4, HGRN_WIDTH), jnp.float32),
            pltpu.VMEM((HGRN_BLOCK, HGRN_WIDTH), jnp.bfloat16),
            pltpu.VMEM((HGRN_WIDTH // LANES, HGRN_BLOCK, HGRN_BLOCK // HGRN_CHUNK * LANES),
                       jnp.bfloat16),
            pltpu.VMEM((max(SUBLANES, HGRN_BLOCK // HGRN_CHUNK), HGRN_WIDTH), jnp.float32),
            pltpu.VMEM((3, HGRN_WIDTH // LANES, HGRN_BLOCK, LANES), jnp.float32),
        ],
        compiler_params=pltpu.CompilerParams(
            dimension_semantics=("arbitrary",), vmem_limit_bytes=VMEM_LIMIT),
        name="hgrn2_pool",
    )(lower_bounds, gn, pool_w_blockdiag, pool_scale, pa)


def _fox_kernel(q_ref, k_ref, v_ref, g_ref, qx_ref, kx_ref, o_ref, kcat_ref, qcat_ref, vaug_ref,
                acc_ref, st_ref):
    first_pair = pl.program_id(1) * ATT_PAIRS
    t_len = k_ref.shape[1]
    bq = ATT_BLOCK
    tile = ATT_DIAG_TILE
    n_qblocks = t_len // bq

    lane = lax.broadcasted_iota(jnp.int32, (t_len, LANES), 1)
    qx = qx_ref[0]
    for pp in range(ATT_PAIRS):
        lanes = slice(pp * LANES, (pp + 1) * LANES)
        kcat_ref[pp, :, 0:LANES] = k_ref[0, :, lanes]
        kcat_ref[pp, :, LANES:2 * LANES] = kx_ref[0]
        vt = v_ref[0, :, lanes].astype(jnp.float32).T.astype(jnp.bfloat16)
        qf = q_ref[0, :, lanes].astype(jnp.float32) * (HEAD_DIM ** -0.5 * LOG2E)
        for h2 in range(2):
            vaug_ref[pp, h2, 0:HEAD_DIM, :] = vt[h2 * HEAD_DIM:(h2 + 1) * HEAD_DIM, :]
            vaug_ref[pp, h2, HEAD_DIM:, :] = jnp.ones((ATT_ONES_ROWS, t_len), jnp.bfloat16)
            qcat_ref[pp, h2, :, 0:LANES] = jnp.where(
                (lane // HEAD_DIM) == h2, qf, 0.0).astype(jnp.bfloat16)
            qcat_ref[pp, h2, :, LANES:2 * LANES] = jnp.where(
                (lane // BIAS_SLOT) == (first_pair + pp) * 2 + h2, qx, jnp.zeros_like(qx))
    acc_ref[...] = jnp.zeros_like(acc_ref)

    def pair_units(pp):
        out = []
        for qb in range(n_qblocks):
            for k0 in range(0, qb * bq, ATT_KEY_TILE):
                out += [(pp, qb, h2, qb * bq, bq, k0, min(ATT_KEY_TILE, qb * bq - k0), False)
                        for h2 in range(2)]
            for h2 in range(2):
                out += [(pp, qb, h2, qb * bq + c * tile, tile, qb * bq, (c + 1) * tile, True)
                        for c in range(bq // tile)]
        return out

    units = [u for group in zip(*[pair_units(pp) for pp in range(ATT_PAIRS)]) for u in group]

    def scores(slot, unit):
        pp, _, h2, q0, nq, k0, nk, _ = unit
        st_ref[slot, 0:nk, 0:nq] = lax.dot_general(
            kcat_ref[pp, k0:k0 + nk, :], qcat_ref[pp, h2, q0:q0 + nq, :], (((1,), (1,)), ((), ())),
            preferred_element_type=jnp.float32)

    m_run = {(pp, qb, h2, c): jnp.full((1, tile), MASK_VALUE, jnp.float32)
             for pp in range(ATT_PAIRS) for qb in range(n_qblocks) for h2 in range(2)
             for c in range(bq // tile)}

    def consume(slot, unit):
        pp, qb, h2, q0, nq, k0, nk, masked = unit
        if masked:
            key = lax.broadcasted_iota(jnp.int32, (nk, nq), 0) + k0
            query = lax.broadcasted_iota(jnp.int32, (nk, nq), 1) + q0
            st_ref[slot, 0:nk, 0:nq] = jnp.where(query >= key, st_ref[slot, 0:nk, 0:nq], MASK_VALUE)
        tiles = [(q0 - qb * bq) // tile + i for i in range(nq // tile)]
        m_prev = jnp.concatenate([m_run[(pp, qb, h2, c)] for c in tiles], axis=1)
        m_new = jnp.maximum(m_prev, jnp.max(st_ref[slot, 0:nk, 0:nq], axis=0, keepdims=True))
        pt = jnp.exp2(st_ref[slot, 0:nk, 0:nq] - m_new).astype(jnp.bfloat16)
        alpha = jnp.exp2(m_prev - m_new)
        cols = slice(q0 - qb * bq, q0 - qb * bq + nq)
        acc_ref[pp, qb, h2, :, cols] = acc_ref[pp, qb, h2, :, cols] * alpha + jnp.dot(
            vaug_ref[pp, h2, :, k0:k0 + nk], pt, preferred_element_type=jnp.float32)
        for i, c in enumerate(tiles):
            m_run[(pp, qb, h2, c)] = m_new[:, i * tile:(i + 1) * tile]

    def finish(pp, qb):
        ot = jnp.concatenate(
            [acc_ref[pp, qb, h2, 0:HEAD_DIM, :] / acc_ref[pp, qb, h2, HEAD_DIM:HEAD_DIM + 1, :]
             for h2 in range(2)], axis=0)
        rows = slice(qb * bq, (qb + 1) * bq)
        lanes = slice(pp * LANES, (pp + 1) * LANES)
        gate = _silu(g_ref[0, rows, lanes].astype(jnp.float32))
        o_ref[0, rows, lanes] = (ot.T * gate).astype(o_ref.dtype)

    ahead = ATT_SLOTS - 1
    for u in range(min(ahead, len(units))):
        scores(u % ATT_SLOTS, units[u])
    last_of_block = {}
    for u, unit in enumerate(units):
        last_of_block[unit[0:2]] = u
    for u, unit in enumerate(units):
        if u + ahead < len(units):
            scores((u + ahead) % ATT_SLOTS, units[u + ahead])
        consume(u % ATT_SLOTS, unit)
        if last_of_block[unit[0:2]] == u:
            finish(*unit[0:2])


def _fox(pc, qx, kx):
    b, t, _ = pc.shape
    width = ATT_PAIRS * LANES
    steps = FOX_WIDTH // width
    section = lambda sec: pl.BlockSpec((1, t, width), lambda i, p: (i, 0, sec * steps + p))
    shared = pl.BlockSpec((1, t, LANES), lambda i, p: (i, 0, 0))
    v_rows = HEAD_DIM + ATT_ONES_ROWS
    return pl.pallas_call(
        _fox_kernel,
        grid=(b, steps),
        in_specs=[section(0), section(1), section(2), section(3), shared, shared],
        out_specs=pl.BlockSpec((1, t, width), lambda i, p: (i, 0, p)),
        out_shape=jax.ShapeDtypeStruct((b, t, FOX_WIDTH), jnp.bfloat16),
        scratch_shapes=[
            pltpu.VMEM((ATT_PAIRS, t, 2 * LANES), jnp.bfloat16),
            pltpu.VMEM((ATT_PAIRS, 2, t, 2 * LANES), jnp.bfloat16),
            pltpu.VMEM((ATT_PAIRS, 2, v_rows, t), jnp.bfloat16),
            pltpu.VMEM((ATT_PAIRS, t // ATT_BLOCK, 2, v_rows, ATT_BLOCK), jnp.float32),
            pltpu.VMEM((ATT_SLOTS, max(ATT_KEY_TILE, ATT_BLOCK), ATT_BLOCK), jnp.float32),
        ],
        compiler_params=pltpu.CompilerParams(
            dimension_semantics=("arbitrary", "arbitrary"), vmem_limit_bytes=VMEM_LIMIT),
        name="fox_attention",
    )(pc, pc, pc, pc, qx, kx)


def _project_out(x_ref, ab_ref, c_ref, w_ref, g_ref):
    mixed = jnp.concatenate([ab_ref[...], c_ref[...]], axis=-1)
    y = jnp.dot(mixed, w_ref[...], preferred_element_type=jnp.float32)
    ms = jnp.mean(y * y, axis=-1, keepdims=True)
    return x_ref[...] + y * lax.rsqrt(ms + NORM_EPS) * g_ref[...]


def _outproj_kernel(x_ref, ab_ref, c_ref, w_ref, g_ref, o_ref):
    o_ref[...] = _project_out(x_ref, ab_ref, c_ref, w_ref, g_ref)


def _outproj(x2d, o_ab, o_c, w, g):
    m = x2d.shape[0]
    return pl.pallas_call(
        _outproj_kernel,
        grid=(m // OUT_ROWS,),
        in_specs=[
            _row_spec(D_MODEL, OUT_ROWS), _row_spec(HGRN_WIDTH + POOL_WIDTH, OUT_ROWS),
            _row_spec(FOX_WIDTH, OUT_ROWS),
            _whole_spec((D_MODEL, D_MODEL)), _whole_spec((1, D_MODEL)),
        ],
        out_specs=_row_spec(D_MODEL, OUT_ROWS),
        out_shape=jax.ShapeDtypeStruct((m, D_MODEL), jnp.float32),
        compiler_params=pltpu.CompilerParams(
            dimension_semantics=("arbitrary",), vmem_limit_bytes=VMEM_LIMIT),
        name="outproj",
    )(x2d, o_ab, o_c, w, g)


def kernel(x, lower_bounds, pre_norm_g, w_in, hgrn_norm_g, fox_f_bias, pool_w, pool_scale, w_out,
           post_norm_g):
    b, t, d = x.shape
    m = b * t
    x2d = x.reshape(m, d)
    w_in_t = jnp.swapaxes(w_in, 1, 2)
    for layer in range(DEPTH):
        bias_pad = jnp.pad(fox_f_bias[layer], (0, LANES - FOX_HEADS)).reshape(1, LANES)
        pool_bd = jax.scipy.linalg.block_diag(*pool_w[layer]).astype(jnp.bfloat16)

        pa, pc, pf = _inproj(x2d, pre_norm_g[layer].reshape(1, d), w_in_t, layer)
        qx, kx = _forget_cumsum(pf.reshape(b, t, LANES), bias_pad)
        o_ab = _hgrn_pool(pa.reshape(b, t, COL_Q), lower_bounds,
                          hgrn_norm_g[layer].reshape(1, HGRN_WIDTH), pool_bd,
                          pool_scale[layer].reshape(1, POOL_WIDTH), layer)
        o_c = _fox(pc.reshape(b, t, COL_FC - COL_Q), qx, kx)
        x2d = _outproj(x2d, o_ab.reshape(m, HGRN_WIDTH + POOL_WIDTH), o_c.reshape(m, FOX_WIDTH),
                       w_out[layer].astype(jnp.bfloat16), post_norm_g[layer].reshape(1, d))
    return x2d.reshape(b, t, d)
```

```python
import functools
import math

import jax
import jax.numpy as jnp
from jax import lax
from jax.experimental import pallas as pl
from jax.experimental.pallas import tpu as pltpu

D_MODEL = 1024
DEPTH = 2
NORM_EPS = 1e-6
MASK_VALUE = -1e30
TINY = 1e-30

HGRN_HEADS = 4
HGRN_WIDTH = 256
HEAD_DIM = 64
POOL_WINDOWS = (2, 4, 8, 16)
POOL_WIDTH = 256
FOX_HEADS = 8
FOX_WIDTH = 512
IN_WIDTH = 3592

LANES = 128
SUBLANES = 8
IN_WIDTH_PAD = 29 * LANES
VMEM_LIMIT = 56 * 1024 * 1024

COL_HGRN = 0
COL_POOL = 1024
COL_Q = 1536
COL_FC = 3584

HGRN_CHUNK = 16
HGRN_BLOCK = 256
HGRN_UNROLL = 8
PROJ_ROWS = 512
OUT_ROWS = 1024
ATT_BLOCK = 512
ATT_KEY_TILE = 512
ATT_DIAG_TILE = 256
ATT_PAIRS = 1
ATT_SLOTS = 4
ATT_ONES_ROWS = 16
CUM_BLOCK = 256
POOL_HALO = 16

LOG2E = math.log2(math.e)
BIAS_SLOT = 8


def _sigmoid_pair(z):
    e = jnp.exp(-jnp.abs(z))
    r = 1.0 / (1.0 + e)
    big, small = r, e * r
    pos = z >= 0
    return jnp.where(pos, big, small), jnp.where(pos, small, big)


def _silu(x):
    return x * (1.0 / (1.0 + jnp.exp2(x * (-LOG2E))))


def _head_indicator(n, dtype):
    r = lax.broadcasted_iota(jnp.int32, (n, n), 0) // HEAD_DIM
    c = lax.broadcasted_iota(jnp.int32, (n, n), 1) // HEAD_DIM
    return (r == c).astype(dtype)


PROJ_WIDTHS = (COL_Q, COL_FC - COL_Q, IN_WIDTH_PAD - COL_FC)
PROJ_DTYPES = (jnp.float32, jnp.bfloat16, jnp.float32)


def _inproj_kernel(x_ref, g_ref, wt_ref, pa_ref, pc_ref, pf_ref):
    x = x_ref[...]
    ms = jnp.mean(x * x, axis=-1, keepdims=True)
    h = (x * lax.rsqrt(ms + NORM_EPS) * g_ref[...]).astype(jnp.bfloat16)

    def project(w_rows):
        return lax.dot_general(h, w_rows.astype(jnp.bfloat16), (((1,), (1,)), ((), ())),
                               preferred_element_type=jnp.float32)

    pa_ref[...] = project(wt_ref[0, 0:COL_Q, :])
    pc_ref[...] = project(wt_ref[0, COL_Q:COL_FC, :]).astype(jnp.bfloat16)
    forget_rows = jnp.concatenate(
        [wt_ref[0, COL_FC:IN_WIDTH, :],
         jnp.zeros((IN_WIDTH_PAD - IN_WIDTH, D_MODEL), jnp.float32)], axis=0)
    pf_ref[...] = project(forget_rows)


def _row_spec(width, rows=PROJ_ROWS):
    return pl.BlockSpec((rows, width), lambda i: (i, 0))


def _whole_spec(shape):
    return pl.BlockSpec(shape, lambda i: (0,) * len(shape))


def _inproj(x2d, g, w_t, layer):
    m = x2d.shape[0]
    return pl.pallas_call(
        _inproj_kernel,
        grid=(m // PROJ_ROWS,),
        in_specs=[
            _row_spec(D_MODEL), _whole_spec((1, D_MODEL)),
            pl.BlockSpec((1, IN_WIDTH, D_MODEL), lambda i: (layer, 0, 0),
                         pipeline_mode=pl.Buffered(1)),
        ],
        out_specs=[_row_spec(wd) for wd in PROJ_WIDTHS],
        out_shape=[jax.ShapeDtypeStruct((m, wd), dt) for wd, dt in zip(PROJ_WIDTHS, PROJ_DTYPES)],
        compiler_params=pltpu.CompilerParams(
            dimension_semantics=("arbitrary",), vmem_limit_bytes=VMEM_LIMIT),
        name="inproj",
    )(x2d, g, w_t)


def _pack_pieces(x):
    packed = jnp.zeros_like(x)
    for part in range(3):
        piece = x.astype(jnp.bfloat16).astype(jnp.float32)
        x = x - piece
        packed = packed + (piece if part == 0 else pltpu.roll(piece, part * FOX_HEADS, axis=1))
    return packed.astype(jnp.bfloat16)


def _cumsum_kernel(f_ref, bias_ref, qx_ref, kx_ref):
    t_len = f_ref.shape[1]
    r = lax.broadcasted_iota(jnp.int32, (CUM_BLOCK, CUM_BLOCK), 0)
    c = lax.broadcasted_iota(jnp.int32, (CUM_BLOCK, CUM_BLOCK), 1)
    tri = (r >= c).astype(jnp.bfloat16)
    lane = lax.broadcasted_iota(jnp.int32, (CUM_BLOCK, LANES), 1)
    head_lane = lane < FOX_HEADS
    slot = lane % BIAS_SLOT
    used = lane < FOX_HEADS * BIAS_SLOT
    ones_q = (used & (slot >= 3) & (slot < 6)).astype(jnp.float32)
    ones_k = (used & (slot < 3)).astype(jnp.float32)
    pr = lax.broadcasted_iota(jnp.int32, (LANES, 2 * LANES), 0)
    pc = lax.broadcasted_iota(jnp.int32, (LANES, 2 * LANES), 1)
    part, head = pr // FOX_HEADS, pr % FOX_HEADS
    valid = pr < 3 * FOX_HEADS
    place = (jnp.where(valid & (pc == head * BIAS_SLOT + part), 1.0, 0.0)
             - jnp.where(valid & (pc == LANES + head * BIAS_SLOT + 3 + part), 1.0, 0.0)
             ).astype(jnp.bfloat16)
    carry = jnp.zeros((1, LANES), jnp.float32)
    for blk in range(t_len // CUM_BLOCK):
        rows = pl.ds(blk * CUM_BLOCK, CUM_BLOCK)
        v = f_ref[0, rows, :] + bias_ref[...]
        logf = jnp.where(head_lane, jnp.minimum(v, 0.0) - jnp.log1p(jnp.exp(-jnp.abs(v))), 0.0)
        sums = jnp.dot(tri, _pack_pieces(logf), preferred_element_type=jnp.float32)
        total = sums + pltpu.roll(sums, LANES - FOX_HEADS, axis=1) + pltpu.roll(
            sums, LANES - 2 * FOX_HEADS, axis=1)
        cs = jnp.where(head_lane, total, 0.0) + carry
        carry = cs[CUM_BLOCK - 1:CUM_BLOCK, :]
        spread = jnp.dot(_pack_pieces(cs * LOG2E), place, preferred_element_type=jnp.float32)
        qx_ref[0, rows, :] = (spread[:, 0:LANES] + ones_q).astype(jnp.bfloat16)
        kx_ref[0, rows, :] = (spread[:, LANES:] + ones_k).astype(jnp.bfloat16)


def _forget_cumsum(pf, bias_pad):
    b, t, _ = pf.shape
    spec = pl.BlockSpec((1, t, LANES), lambda i: (i, 0, 0))
    return pl.pallas_call(
        _cumsum_kernel,
        grid=(b,),
        in_specs=[spec, pl.BlockSpec((1, LANES), lambda i: (0, 0))],
        out_specs=[spec, spec],
        out_shape=[jax.ShapeDtypeStruct((b, t, LANES), jnp.bfloat16)] * 2,
        compiler_params=pltpu.CompilerParams(dimension_semantics=("arbitrary",)),
        name="forget_cumsum",
    )(pf, bias_pad)


def _hgrn_pool_kernel(lb_ref, gn_ref, pw_ref, ps_ref, proj_ref, o_ref, st_ref, *slots, layer):
    t_len = proj_ref.shape[1]
    w = HGRN_WIDTH
    c_len = HGRN_CHUNK
    half = c_len // 2
    rb = HGRN_BLOCK
    n_chunks = rb // c_len
    pairs = w // LANES

    raw = lb_ref[...]
    e = jnp.exp(raw - jnp.max(raw, axis=0, keepdims=True))
    p = e / jnp.sum(e, axis=0, keepdims=True)
    lb = jnp.sum(p[0:layer + 1, :], axis=0, keepdims=True) - p[0:1, :]

    ind = _head_indicator(w, jnp.bfloat16)
    r = lax.broadcasted_iota(jnp.int32, (rb, rb), 0)
    c = lax.broadcasted_iota(jnp.int32, (rb, rb), 1)
    tri = ((r >= c) & (r // c_len == c // c_len)).astype(jnp.bfloat16)
    row_tile = lax.broadcasted_iota(jnp.int32, (half, w), 0)
    first_head = lax.broadcasted_iota(jnp.int32, (HEAD_DIM, LANES), 1) < HEAD_DIM
    pool_row = lax.broadcasted_iota(jnp.int32, (rb, w), 0)
    pool_group = lax.broadcasted_iota(jnp.int32, (rb, w), 1) // HEAD_DIM
    gn = gn_ref[...]

    st_ref[...] = jnp.zeros_like(st_ref)
    n_blocks = t_len // rb
    ROW_B, ROW_K, ROW_V = range(3)

    def put_rows(rows_ref, which, x):
        for pr in range(pairs):
            rows_ref[which, pr] = x[:, pr * LANES:(pr + 1) * LANES]

    def row_tile_of(rows_ref, which, r):
        return jnp.concatenate([rows_ref[which, pr, pl.ds(r, half, stride=0), :]
                                for pr in range(pairs)], axis=1)

    def front(i, slot):
        a_ref, qd_ref, kd_ref, dl_ref, rows_ref = slot
        r0 = pl.multiple_of(i * rb, rb)
        rows = pl.ds(r0, rb)
        q = _silu(proj_ref[0, rows, 0:w])
        z = proj_ref[0, rows, w:2 * w]

        sig, sig_neg = _sigmoid_pair(z)
        f = lb + (1.0 - lb) * sig
        logf = jnp.log(jnp.maximum(f, TINY)) * LOG2E
        k = (1.0 - lb) * sig_neg
        b = jnp.zeros((rb, w), jnp.float32)
        rest = logf
        for _ in range(3):
            piece = rest.astype(jnp.bfloat16)
            rest = rest - piece.astype(jnp.float32)
            b = b + jnp.dot(tri, piece, preferred_element_type=jnp.float32)
        put_rows(rows_ref, ROW_B, b)
        put_rows(rows_ref, ROW_K, k)
        put_rows(rows_ref, ROW_V, proj_ref[0, rows, 2 * w:3 * w])
        qd_ref[...] = (q * jnp.exp2(b)).astype(jnp.bfloat16)
        kd_ref[...] = jnp.zeros_like(kd_ref)
        for ch in range(n_chunks):
            lo = ch * c_len
            b_last = row_tile_of(rows_ref, ROW_B, lo + c_len - 1)
            kd = jnp.concatenate(
                [k[t0:t0 + half, :] * jnp.exp2(b_last - b[t0:t0 + half, :]) for t0 in (lo, lo + half)],
                axis=0).astype(jnp.bfloat16)
            for pr in range(pairs):
                kd_ref[pr, lo:lo + c_len, ch * LANES:(ch + 1) * LANES] = kd[:, pr * LANES:(pr + 1) * LANES]
            dl_ref[ch:ch + 1, :] = jnp.exp2(b_last[0:1, :])

        parts = []
        for ch in range(n_chunks):
            lo = ch * c_len
            for s in range(c_len):
                bs = row_tile_of(rows_ref, ROW_B, lo + s)
                ks = row_tile_of(rows_ref, ROW_K, lo + s)
                for tl in range(s // half, c_len // half):
                    t0 = lo + tl * half
                    dec = jnp.exp2(b[t0:t0 + half, :] - bs)
                    if tl * half < s:
                        dec = jnp.where(row_tile + tl * half >= s, dec, 0.0)
                    parts.append(q[t0:t0 + half, :] * (ks * dec))
        n_half = len(parts) // 2
        for lo_p, hi_p in ((0, n_half), (n_half, len(parts))):
            a = jnp.concatenate(parts[lo_p:hi_p], axis=0).astype(jnp.bfloat16)
            a_ref[lo_p * half:hi_p * half, :] = jnp.dot(a, ind, preferred_element_type=jnp.float32)

    def back(i, slot):
        a_ref, qd_ref, kd_ref, dl_ref, rows_ref = slot
        r0 = pl.multiple_of(i * rb, rb)
        rows = pl.ds(r0, rb)
        gate = proj_ref[0, rows, 3 * w:4 * w]
        v_t = proj_ref[0, rows, 2 * w:3 * w].T.astype(jnp.bfloat16)
        upd_all = [jnp.dot(v_t[pr * LANES:(pr + 1) * LANES, :], kd_ref[pr],
                           preferred_element_type=jnp.float32)
                   for pr in range(pairs)]
        outs = []
        off = 0
        for ch in range(n_chunks):
            lo = ch * c_len
            o_tiles = [jnp.zeros((half, w), jnp.float32) for _ in range(c_len // half)]
            for s in range(c_len):
                vs = row_tile_of(rows_ref, ROW_V, lo + s)
                for tl in range(s // half, c_len // half):
                    o_tiles[tl] = o_tiles[tl] + a_ref[off:off + half, :] * vs
                    off += half

            dec_state = dl_ref[ch:ch + 1, :]
            inter = []
            for pr in range(pairs):
                lanes = slice(pr * LANES, (pr + 1) * LANES)
                st = st_ref[pr]
                wmat = jnp.concatenate([jnp.where(first_head, st, 0.0),
                                        jnp.where(first_head, 0.0, st)], axis=0).astype(jnp.bfloat16)
                inter.append(lax.dot_general(qd_ref[lo:lo + c_len, lanes], wmat,
                                             (((1,), (1,)), ((), ())),
                                             preferred_element_type=jnp.float32))
                upd = upd_all[pr][:, ch * LANES:(ch + 1) * LANES]
                st_ref[pr] = st * dec_state[:, lanes] + jnp.where(
                    first_head, upd[0:HEAD_DIM, :], upd[HEAD_DIM:, :])
            outs.append(jnp.concatenate(o_tiles, axis=0) + jnp.concatenate(inter, axis=1))
        o = jnp.concatenate(outs, axis=0)

        ms = jnp.dot((o * o).astype(jnp.bfloat16), ind,
                     preferred_element_type=jnp.float32) * (1.0 / HEAD_DIM)
        y = o * lax.rsqrt(ms + NORM_EPS) * gn
        o_ref[0, rows, 0:w] = (y * _silu(gate)).astype(o_ref.dtype)

        h0 = pl.multiple_of(jnp.maximum(r0 - POOL_HALO, 0), POOL_HALO)
        halo = proj_ref[0, pl.ds(h0, POOL_HALO), COL_POOL:COL_POOL + w]
        u = proj_ref[0, rows, COL_POOL:COL_POOL + w]
        x = jnp.concatenate([jnp.where(i > 0, halo, 0.0), u], axis=0)
        s2 = x + pltpu.roll(x, 1, axis=0)
        s4 = s2 + pltpu.roll(s2, 2, axis=0)
        s8 = s4 + pltpu.roll(s4, 4, axis=0)
        s16 = s8 + pltpu.roll(s8, 8, axis=0)
        tpos = (pool_row + r0 + 1).astype(jnp.float32)
        pooled = None
        for gi, (win, sm) in enumerate(zip(POOL_WINDOWS, (s2, s4, s8, s16))):
            val = sm[POOL_HALO:, :] / jnp.minimum(tpos, float(win))
            pooled = val if pooled is None else jnp.where(pool_group == gi, val, pooled)
        mixed = jnp.dot((pooled - u).astype(jnp.bfloat16), pw_ref[...],
                        preferred_element_type=jnp.float32)
        pool_gate = _silu(proj_ref[0, rows, COL_POOL + w:COL_POOL + 2 * w])
        o_ref[0, rows, w:2 * w] = ((mixed * ps_ref[...]) * pool_gate).astype(o_ref.dtype)

    slot_of = (slots[0:5], slots[5:10])
    front(0, slot_of[0])

    def several_blocks(j, carry):
        i0 = HGRN_UNROLL * j
        for u in range(HGRN_UNROLL):
            front(jnp.minimum(i0 + u + 1, n_blocks - 1), slot_of[(u + 1) % 2])
            back(i0 + u, slot_of[u % 2])
        return carry

    lax.fori_loop(0, n_blocks // HGRN_UNROLL, several_blocks, 0)


def _hgrn_pool(pa, lower_bounds, gn, pool_w_blockdiag, pool_scale, layer):
    b, t, width = pa.shape
    out_width = HGRN_WIDTH + POOL_WIDTH
    return pl.pallas_call(
        functools.partial(_hgrn_pool_kernel, layer=layer),
        grid=(b,),
        in_specs=[
            pl.BlockSpec((DEPTH, HGRN_WIDTH), lambda i: (0, 0)),
            pl.BlockSpec((1, HGRN_WIDTH), lambda i: (0, 0)),
            pl.BlockSpec((POOL_WIDTH, POOL_WIDTH), lambda i: (0, 0)),
            pl.BlockSpec((1, POOL_WIDTH), lambda i: (0, 0)),
            pl.BlockSpec((1, t, width), lambda i: (i, 0, 0)),
        ],
        out_specs=pl.BlockSpec((1, t, out_width), lambda i: (i, 0, 0)),
        out_shape=jax.ShapeDtypeStruct((b, t, out_width), jnp.bfloat16),
        scratch_shapes=[pltpu.VMEM((HGRN_WIDTH // LANES, HEAD_DIM, LANES), jnp.float32)] + 2 * [
            pltpu.VMEM((HGRN_BLOCK * 3 * HGRN_CHUNK // 4, HGRN_WIDTH), jnp.float32),
            pltpu.VMEM((HGRN_BLOCK, HGRN_WIDTH), jnp.bfloat16),
            pltpu.VMEM((HGRN_WIDTH // LANES, HGRN_BLOCK, HGRN_BLOCK // HGRN_CHUNK * LANES),
                       jnp.bfloat16),
            pltpu.VMEM((max(SUBLANES, HGRN_BLOCK // HGRN_CHUNK), HGRN_WIDTH), jnp.float32),
            pltpu.VMEM((3, HGRN_WIDTH // LANES, HGRN_BLOCK, LANES), jnp.float32),
        ],
        compiler_params=pltpu.CompilerParams(
            dimension_semantics=("arbitrary",), vmem_limit_bytes=VMEM_LIMIT),
        name="hgrn2_pool",
    )(lower_bounds, gn, pool_w_blockdiag, pool_scale, pa)


def _fox_kernel(q_ref, k_ref, v_ref, g_ref, qx_ref, kx_ref, o_ref, kcat_ref, qcat_ref, vaug_ref,
                acc_ref, st_ref):
    first_pair = pl.program_id(1) * ATT_PAIRS
    t_len = k_ref.shape[1]
    bq = ATT_BLOCK
    tile = ATT_DIAG_TILE
    n_qblocks = t_len // bq

    lane = lax.broadcasted_iota(jnp.int32, (t_len, LANES), 1)
    qx = qx_ref[0]
    for pp in range(ATT_PAIRS):
        lanes = slice(pp * LANES, (pp + 1) * LANES)
        kcat_ref[pp, :, 0:LANES] = k_ref[0, :, lanes]
        kcat_ref[pp, :, LANES:2 * LANES] = kx_ref[0]
        vt = v_ref[0, :, lanes].astype(jnp.float32).T.astype(jnp.bfloat16)
        qf = q_ref[0, :, lanes].astype(jnp.float32) * (HEAD_DIM ** -0.5 * LOG2E)
        for h2 in range(2):
            vaug_ref[pp, h2, 0:HEAD_DIM, :] = vt[h2 * HEAD_DIM:(h2 + 1) * HEAD_DIM, :]
            vaug_ref[pp, h2, HEAD_DIM:, :] = jnp.ones((ATT_ONES_ROWS, t_len), jnp.bfloat16)
            qcat_ref[pp, h2, :, 0:LANES] = jnp.where(
                (lane // HEAD_DIM) == h2, qf, 0.0).astype(jnp.bfloat16)
            qcat_ref[pp, h2, :, LANES:2 * LANES] = jnp.where(
                (lane // BIAS_SLOT) == (first_pair + pp) * 2 + h2, qx, jnp.zeros_like(qx))
    acc_ref[...] = jnp.zeros_like(acc_ref)

    def pair_units(pp):
        out = []
        for qb in range(n_qblocks):
            for k0 in range(0, qb * bq, ATT_KEY_TILE):
                out += [(pp, qb, h2, qb * bq, bq, k0, min(ATT_KEY_TILE, qb * bq - k0), False)
                        for h2 in range(2)]
            for h2 in range(2):
                out += [(pp, qb, h2, qb * bq + c * tile, tile, qb * bq, (c + 1) * tile, True)
                        for c in range(bq // tile)]
        return out

    units = [u for group in zip(*[pair_units(pp) for pp in range(ATT_PAIRS)]) for u in group]

    def scores(slot, unit):
        pp, _, h2, q0, nq, k0, nk, _ = unit
        st_ref[slot, 0:nk, 0:nq] = lax.dot_general(
            kcat_ref[pp, k0:k0 + nk, :], qcat_ref[pp, h2, q0:q0 + nq, :], (((1,), (1,)), ((), ())),
            preferred_element_type=jnp.float32)

    m_run = {(pp, qb, h2, c): jnp.full((1, tile), MASK_VALUE, jnp.float32)
             for pp in range(ATT_PAIRS) for qb in range(n_qblocks) for h2 in range(2)
             for c in range(bq // tile)}

    def consume(slot, unit):
        pp, qb, h2, q0, nq, k0, nk, masked = unit
        if masked:
            key = lax.broadcasted_iota(jnp.int32, (nk, nq), 0) + k0
            query = lax.broadcasted_iota(jnp.int32, (nk, nq), 1) + q0
            st_ref[slot, 0:nk, 0:nq] = jnp.where(query >= key, st_ref[slot, 0:nk, 0:nq], MASK_VALUE)
        tiles = [(q0 - qb * bq) // tile + i for i in range(nq // tile)]
        m_prev = jnp.concatenate([m_run[(pp, qb, h2, c)] for c in tiles], axis=1)
        m_new = jnp.maximum(m_prev, jnp.max(st_ref[slot, 0:nk, 0:nq], axis=0, keepdims=True))
        pt = jnp.exp2(st_ref[slot, 0:nk, 0:nq] - m_new).astype(jnp.bfloat16)
        alpha = jnp.exp2(m_prev - m_new)
        cols = slice(q0 - qb * bq, q0 - qb * bq + nq)
        acc_ref[pp, qb, h2, :, cols] = acc_ref[pp, qb, h2, :, cols] * alpha + jnp.dot(
            vaug_ref[pp, h2, :, k0:k0 + nk], pt, preferred_element_type=jnp.float32)
        for i, c in enumerate(tiles):
            m_run[(pp, qb, h2, c)] = m_new[:, i * tile:(i + 1) * tile]

    def finish(pp, qb):
        ot = jnp.concatenate(
            [acc_ref[pp, qb, h2, 0:HEAD_DIM, :] / acc_ref[pp, qb, h2, HEAD_DIM:HEAD_DIM + 1, :]
             for h2 in range(2)], axis=0)
        rows = slice(qb * bq, (qb + 1) * bq)
        lanes = slice(pp * LANES, (pp + 1) * LANES)
        gate = _silu(g_ref[0, rows, lanes].astype(jnp.float32))
        o_ref[0, rows, lanes] = (ot.T * gate).astype(o_ref.dtype)

    ahead = ATT_SLOTS - 1
    for u in range(min(ahead, len(units))):
        scores(u % ATT_SLOTS, units[u])
    last_of_block = {}
    for u, unit in enumerate(units):
        last_of_block[unit[0:2]] = u
    for u, unit in enumerate(units):
        if u + ahead < len(units):
            scores((u + ahead) % ATT_SLOTS, units[u + ahead])
        consume(u % ATT_SLOTS, unit)
        if last_of_block[unit[0:2]] == u:
            finish(*unit[0:2])


def _fox(pc, qx, kx):
    b, t, _ = pc.shape
    width = ATT_PAIRS * LANES
    steps = FOX_WIDTH // width
    section = lambda sec: pl.BlockSpec((1, t, width), lambda i, p: (i, 0, sec * steps + p))
    shared = pl.BlockSpec((1, t, LANES), lambda i, p: (i, 0, 0))
    v_rows = HEAD_DIM + ATT_ONES_ROWS
    return pl.pallas_call(
        _fox_kernel,
        grid=(b, steps),
        in_specs=[section(0), section(1), section(2), section(3), shared, shared],
        out_specs=pl.BlockSpec((1, t, width), lambda i, p: (i, 0, p)),
        out_shape=jax.ShapeDtypeStruct((b, t, FOX_WIDTH), jnp.bfloat16),
        scratch_shapes=[
            pltpu.VMEM((ATT_PAIRS, t, 2 * LANES), jnp.bfloat16),
            pltpu.VMEM((ATT_PAIRS, 2, t, 2 * LANES), jnp.bfloat16),
            pltpu.VMEM((ATT_PAIRS, 2, v_rows, t), jnp.bfloat16),
            pltpu.VMEM((ATT_PAIRS, t // ATT_BLOCK, 2, v_rows, ATT_BLOCK), jnp.float32),
            pltpu.VMEM((ATT_SLOTS, max(ATT_KEY_TILE, ATT_BLOCK), ATT_BLOCK), jnp.float32),
        ],
        compiler_params=pltpu.CompilerParams(
            dimension_semantics=("arbitrary", "arbitrary"), vmem_limit_bytes=VMEM_LIMIT),
        name="fox_attention",
    )(pc, pc, pc, pc, qx, kx)


def _project_out(x_ref, ab_ref, c_ref, w_ref, g_ref):
    mixed = jnp.concatenate([ab_ref[...], c_ref[...]], axis=-1)
    y = jnp.dot(mixed, w_ref[...], preferred_element_type=jnp.float32)
    ms = jnp.mean(y * y, axis=-1, keepdims=True)
    return x_ref[...] + y * lax.rsqrt(ms + NORM_EPS) * g_ref[...]


def _outproj_kernel(x_ref, ab_ref, c_ref, w_ref, g_ref, o_ref):
    o_ref[...] = _project_out(x_ref, ab_ref, c_ref, w_ref, g_ref)


def _outproj(x2d, o_ab, o_c, w, g):
    m = x2d.shape[0]
    return pl.pallas_call(
        _outproj_kernel,
        grid=(m // OUT_ROWS,),
        in_specs=[
            _row_spec(D_MODEL, OUT_ROWS), _row_spec(HGRN_WIDTH + POOL_WIDTH, OUT_ROWS),
            _row_spec(FOX_WIDTH, OUT_ROWS),
            _whole_spec((D_MODEL, D_MODEL)), _whole_spec((1, D_MODEL)),
        ],
        out_specs=_row_spec(D_MODEL, OUT_ROWS),
        out_shape=jax.ShapeDtypeStruct((m, D_MODEL), jnp.float32),
        compiler_params=pltpu.CompilerParams(
            dimension_semantics=("arbitrary",), vmem_limit_bytes=VMEM_LIMIT),
        name="outproj",
    )(x2d, o_ab, o_c, w, g)


def kernel(x, lower_bounds, pre_norm_g, w_in, hgrn_norm_g, fox_f_bias, pool_w, pool_scale, w_out,
           post_norm_g):
    b, t, d = x.shape
    m = b * t
    x2d = x.reshape(m, d)
    w_in_t = jnp.swapaxes(w_in, 1, 2)
    for layer in range(DEPTH):
        bias_pad = jnp.pad(fox_f_bias[layer], (0, LANES - FOX_HEADS)).reshape(1, LANES)
        pool_bd = jax.scipy.linalg.block_diag(*pool_w[layer]).astype(jnp.bfloat16)

        pa, pc, pf = _inproj(x2d, pre_norm_g[layer].reshape(1, d), w_in_t, layer)
        qx, kx = _forget_cumsum(pf.reshape(b, t, LANES), bias_pad)
        o_ab = _hgrn_pool(pa.reshape(b, t, COL_Q), lower_bounds,
                          hgrn_norm_g[layer].reshape(1, HGRN_WIDTH), pool_bd,
                          pool_scale[layer].reshape(1, POOL_WIDTH), layer)
        o_c = _fox(pc.reshape(b, t, COL_FC - COL_Q), qx, kx)
        x2d = _outproj(x2d, o_ab.reshape(m, HGRN_WIDTH + POOL_WIDTH), o_c.reshape(m, FOX_WIDTH),
                       w_out[layer].astype(jnp.bfloat16), post_norm_g[layer].reshape(1, d))
    return x2d.reshape(b, t, d)
```

```python
import functools
import math

import jax
import jax.numpy as jnp
from jax import lax
from jax.experimental import pallas as pl
from jax.experimental.pallas import tpu as pltpu

D_MODEL = 1024
DEPTH = 2
NORM_EPS = 1e-6
MASK_VALUE = -1e30
TINY = 1e-30

HGRN_HEADS = 4
HGRN_WIDTH = 256
HEAD_DIM = 64
POOL_WINDOWS = (2, 4, 8, 16)
POOL_WIDTH = 256
FOX_HEADS = 8
FOX_WIDTH = 512
IN_WIDTH = 3592

LANES = 128
SUBLANES = 8
IN_WIDTH_PAD = 29 * LANES
VMEM_LIMIT = 56 * 1024 * 1024

COL_HGRN = 0
COL_POOL = 1024
COL_Q = 1536
COL_FC = 3584

HGRN_CHUNK = 16
HGRN_BLOCK = 256
HGRN_UNROLL = 8
PROJ_ROWS = 512
OUT_ROWS = 1024
ATT_BLOCK = 512
ATT_KEY_TILE = 512
ATT_DIAG_TILE = 256
ATT_PAIRS = 1
ATT_SLOTS = 4
ATT_ONES_ROWS = 16
CUM_BLOCK = 256
POOL_HALO = 16

LOG2E = math.log2(math.e)
BIAS_SLOT = 8


def _sigmoid_pair(z):
    t = z * LOG2E
    return 1.0 / (1.0 + jnp.exp2(-t)), 1.0 / (1.0 + jnp.exp2(t))


def _silu(x):
    return x * (1.0 / (1.0 + jnp.exp2(x * (-LOG2E))))


def _head_indicator(n, dtype):
    r = lax.broadcasted_iota(jnp.int32, (n, n), 0) // HEAD_DIM
    c = lax.broadcasted_iota(jnp.int32, (n, n), 1) // HEAD_DIM
    return (r == c).astype(dtype)


PROJ_WIDTHS = (COL_Q, COL_FC - COL_Q, IN_WIDTH_PAD - COL_FC)
PROJ_DTYPES = (jnp.float32, jnp.bfloat16, jnp.float32)


def _inproj_kernel(x_ref, g_ref, wt_ref, pa_ref, pc_ref, pf_ref):
    x = x_ref[...]
    ms = jnp.mean(x * x, axis=-1, keepdims=True)
    h = (x * lax.rsqrt(ms + NORM_EPS) * g_ref[...]).astype(jnp.bfloat16)

    def project(w_rows):
        return lax.dot_general(h, w_rows.astype(jnp.bfloat16), (((1,), (1,)), ((), ())),
                               preferred_element_type=jnp.float32)

    pa_ref[...] = project(wt_ref[0, 0:COL_Q, :])
    pc_ref[...] = project(wt_ref[0, COL_Q:COL_FC, :]).astype(jnp.bfloat16)
    forget_rows = jnp.concatenate(
        [wt_ref[0, COL_FC:IN_WIDTH, :],
         jnp.zeros((IN_WIDTH_PAD - IN_WIDTH, D_MODEL), jnp.float32)], axis=0)
    pf_ref[...] = project(forget_rows)


def _row_spec(width, rows=PROJ_ROWS):
    return pl.BlockSpec((rows, width), lambda i: (i, 0))


def _whole_spec(shape):
    return pl.BlockSpec(shape, lambda i: (0,) * len(shape))


def _inproj(x2d, g, w_t, layer):
    m = x2d.shape[0]
    return pl.pallas_call(
        _inproj_kernel,
        grid=(m // PROJ_ROWS,),
        in_specs=[
            _row_spec(D_MODEL), _whole_spec((1, D_MODEL)),
            pl.BlockSpec((1, IN_WIDTH, D_MODEL), lambda i: (layer, 0, 0),
                         pipeline_mode=pl.Buffered(1)),
        ],
        out_specs=[_row_spec(wd) for wd in PROJ_WIDTHS],
        out_shape=[jax.ShapeDtypeStruct((m, wd), dt) for wd, dt in zip(PROJ_WIDTHS, PROJ_DTYPES)],
        compiler_params=pltpu.CompilerParams(
            dimension_semantics=("arbitrary",), vmem_limit_bytes=VMEM_LIMIT),
        name="inproj",
    )(x2d, g, w_t)


def _pack_pieces(x):
    packed = jnp.zeros_like(x)
    for part in range(3):
        piece = x.astype(jnp.bfloat16).astype(jnp.float32)
        x = x - piece
        packed = packed + (piece if part == 0 else pltpu.roll(piece, part * FOX_HEADS, axis=1))
    return packed.astype(jnp.bfloat16)


def _cumsum_kernel(f_ref, bias_ref, qx_ref, kx_ref):
    t_len = f_ref.shape[1]
    r = lax.broadcasted_iota(jnp.int32, (CUM_BLOCK, CUM_BLOCK), 0)
    c = lax.broadcasted_iota(jnp.int32, (CUM_BLOCK, CUM_BLOCK), 1)
    tri = (r >= c).astype(jnp.bfloat16)
    lane = lax.broadcasted_iota(jnp.int32, (CUM_BLOCK, LANES), 1)
    head_lane = lane < FOX_HEADS
    slot = lane % BIAS_SLOT
    used = lane < FOX_HEADS * BIAS_SLOT
    ones_q = (used & (slot >= 3) & (slot < 6)).astype(jnp.float32)
    ones_k = (used & (slot < 3)).astype(jnp.float32)
    pr = lax.broadcasted_iota(jnp.int32, (LANES, 2 * LANES), 0)
    pc = lax.broadcasted_iota(jnp.int32, (LANES, 2 * LANES), 1)
    part, head = pr // FOX_HEADS, pr % FOX_HEADS
    valid = pr < 3 * FOX_HEADS
    place = (jnp.where(valid & (pc == head * BIAS_SLOT + part), 1.0, 0.0)
             - jnp.where(valid & (pc == LANES + head * BIAS_SLOT + 3 + part), 1.0, 0.0)
             ).astype(jnp.bfloat16)
    carry = jnp.zeros((1, LANES), jnp.float32)
    for blk in range(t_len // CUM_BLOCK):
        rows = pl.ds(blk * CUM_BLOCK, CUM_BLOCK)
        v = f_ref[0, rows, :] + bias_ref[...]
        logf = jnp.where(head_lane, jnp.minimum(v, 0.0) - jnp.log1p(jnp.exp(-jnp.abs(v))), 0.0)
        sums = jnp.dot(tri, _pack_pieces(logf), preferred_element_type=jnp.float32)
        total = sums + pltpu.roll(sums, LANES - FOX_HEADS, axis=1) + pltpu.roll(
            sums, LANES - 2 * FOX_HEADS, axis=1)
        cs = jnp.where(head_lane, total, 0.0) + carry
        carry = cs[CUM_BLOCK - 1:CUM_BLOCK, :]
        spread = jnp.dot(_pack_pieces(cs * LOG2E), place, preferred_element_type=jnp.float32)
        qx_ref[0, rows, :] = (spread[:, 0:LANES] + ones_q).astype(jnp.bfloat16)
        kx_ref[0, rows, :] = (spread[:, LANES:] + ones_k).astype(jnp.bfloat16)


def _forget_cumsum(pf, bias_pad):
    b, t, _ = pf.shape
    spec = pl.BlockSpec((1, t, LANES), lambda i: (i, 0, 0))
    return pl.pallas_call(
        _cumsum_kernel,
        grid=(b,),
        in_specs=[spec, pl.BlockSpec((1, LANES), lambda i: (0, 0))],
        out_specs=[spec, spec],
        out_shape=[jax.ShapeDtypeStruct((b, t, LANES), jnp.bfloat16)] * 2,
        compiler_params=pltpu.CompilerParams(dimension_semantics=("arbitrary",)),
        name="forget_cumsum",
    )(pf, bias_pad)


def _hgrn_pool_kernel(lb_ref, gn_ref, pw_ref, ps_ref, proj_ref, o_ref, st_ref, *slots, layer):
    t_len = proj_ref.shape[1]
    w = HGRN_WIDTH
    c_len = HGRN_CHUNK
    half = c_len // 2
    rb = HGRN_BLOCK
    n_chunks = rb // c_len
    pairs = w // LANES

    raw = lb_ref[...]
    e = jnp.exp(raw - jnp.max(raw, axis=0, keepdims=True))
    p = e / jnp.sum(e, axis=0, keepdims=True)
    lb = jnp.sum(p[0:layer + 1, :], axis=0, keepdims=True) - p[0:1, :]

    ind = _head_indicator(w, jnp.bfloat16)
    r = lax.broadcasted_iota(jnp.int32, (rb, rb), 0)
    c = lax.broadcasted_iota(jnp.int32, (rb, rb), 1)
    tri = ((r >= c) & (r // c_len == c // c_len)).astype(jnp.bfloat16)
    row_tile = lax.broadcasted_iota(jnp.int32, (half, w), 0)
    first_head = lax.broadcasted_iota(jnp.int32, (HEAD_DIM, LANES), 1) < HEAD_DIM
    pool_row = lax.broadcasted_iota(jnp.int32, (rb, LANES), 0)
    pool_first_group = lax.broadcasted_iota(jnp.int32, (rb, LANES), 1) < HEAD_DIM
    gn = gn_ref[...]

    st_ref[...] = jnp.zeros_like(st_ref)
    n_blocks = t_len // rb
    ROW_B, ROW_K, ROW_V = range(3)

    def put_rows(rows_ref, which, x):
        for pr in range(pairs):
            rows_ref[which, pr] = x[:, pr * LANES:(pr + 1) * LANES]

    def row_tile_of(rows_ref, which, r):
        return jnp.concatenate([rows_ref[which, pr, pl.ds(r, half, stride=0), :]
                                for pr in range(pairs)], axis=1)

    def front(i, slot):
        a_ref, qd_ref, kd_ref, dl_ref, rows_ref = slot
        r0 = pl.multiple_of(i * rb, rb)
        rows = pl.ds(r0, rb)
        q = _silu(proj_ref[0, rows, 0:w])
        z = proj_ref[0, rows, w:2 * w]

        sig, sig_neg = _sigmoid_pair(z)
        f = lb + (1.0 - lb) * sig
        logf = jnp.log(jnp.maximum(f, TINY)) * LOG2E
        k = (1.0 - lb) * sig_neg
        b = jnp.zeros((rb, w), jnp.float32)
        rest = logf
        for _ in range(3):
            piece = rest.astype(jnp.bfloat16)
            rest = rest - piece.astype(jnp.float32)
            b = b + jnp.dot(tri, piece, preferred_element_type=jnp.float32)
        put_rows(rows_ref, ROW_B, b)
        put_rows(rows_ref, ROW_K, k)
        put_rows(rows_ref, ROW_V, proj_ref[0, rows, 2 * w:3 * w])
        qd_ref[...] = (q * jnp.exp2(b)).astype(jnp.bfloat16)
        kd_ref[...] = jnp.zeros_like(kd_ref)
        for ch in range(n_chunks):
            lo = ch * c_len
            b_last = row_tile_of(rows_ref, ROW_B, lo + c_len - 1)
            kd = jnp.concatenate(
                [k[t0:t0 + half, :] * jnp.exp2(b_last - b[t0:t0 + half, :]) for t0 in (lo, lo + half)],
                axis=0).astype(jnp.bfloat16)
            for pr in range(pairs):
                kd_ref[pr, lo:lo + c_len, ch * LANES:(ch + 1) * LANES] = kd[:, pr * LANES:(pr + 1) * LANES]
            dl_ref[ch:ch + 1, :] = jnp.exp2(b_last[0:1, :])

        parts = []
        for ch in range(n_chunks):
            lo = ch * c_len
            for s in range(c_len):
                bs = row_tile_of(rows_ref, ROW_B, lo + s)
                ks = row_tile_of(rows_ref, ROW_K, lo + s)
                for tl in range(s // half, c_len // half):
                    t0 = lo + tl * half
                    dec = jnp.exp2(b[t0:t0 + half, :] - bs)
                    if tl * half < s:
                        dec = jnp.where(row_tile + tl * half >= s, dec, 0.0)
                    parts.append(q[t0:t0 + half, :] * (ks * dec))
        n_half = len(parts) // 2
        for lo_p, hi_p in ((0, n_half), (n_half, len(parts))):
            a = jnp.concatenate(parts[lo_p:hi_p], axis=0).astype(jnp.bfloat16)
            a_ref[lo_p * half:hi_p * half, :] = jnp.dot(a, ind, preferred_element_type=jnp.float32)

    def back(i, slot):
        a_ref, qd_ref, kd_ref, dl_ref, rows_ref = slot
        r0 = pl.multiple_of(i * rb, rb)
        rows = pl.ds(r0, rb)
        gate = proj_ref[0, rows, 3 * w:4 * w]
        v_t = proj_ref[0, rows, 2 * w:3 * w].T.astype(jnp.bfloat16)
        upd_all = [jnp.dot(v_t[pr * LANES:(pr + 1) * LANES, :], kd_ref[pr],
                           preferred_element_type=jnp.float32)
                   for pr in range(pairs)]
        outs = []
        off = 0
        for ch in range(n_chunks):
            lo = ch * c_len
            o_tiles = [jnp.zeros((half, w), jnp.float32) for _ in range(c_len // half)]
            for s in range(c_len):
                vs = row_tile_of(rows_ref, ROW_V, lo + s)
                for tl in range(s // half, c_len // half):
                    o_tiles[tl] = o_tiles[tl] + a_ref[off:off + half, :] * vs
                    off += half

            dec_state = dl_ref[ch:ch + 1, :]
            inter = []
            for pr in range(pairs):
                lanes = slice(pr * LANES, (pr + 1) * LANES)
                st = st_ref[pr]
                wmat = jnp.concatenate([jnp.where(first_head, st, 0.0),
                                        jnp.where(first_head, 0.0, st)], axis=0).astype(jnp.bfloat16)
                inter.append(lax.dot_general(qd_ref[lo:lo + c_len, lanes], wmat,
                                             (((1,), (1,)), ((), ())),
                                             preferred_element_type=jnp.float32))
                upd = upd_all[pr][:, ch * LANES:(ch + 1) * LANES]
                st_ref[pr] = st * dec_state[:, lanes] + jnp.where(
                    first_head, upd[0:HEAD_DIM, :], upd[HEAD_DIM:, :])
            outs.append(jnp.concatenate(o_tiles, axis=0) + jnp.concatenate(inter, axis=1))
        o = jnp.concatenate(outs, axis=0)

        ms = jnp.dot((o * o).astype(jnp.bfloat16), ind,
                     preferred_element_type=jnp.float32) * (1.0 / HEAD_DIM)
        y = o * lax.rsqrt(ms + NORM_EPS) * gn
        o_ref[0, rows, 0:w] = (y * _silu(gate)).astype(o_ref.dtype)

        h0 = pl.multiple_of(jnp.maximum(r0 - POOL_HALO, 0), POOL_HALO)
        halo = proj_ref[0, pl.ds(h0, POOL_HALO), COL_POOL:COL_POOL + w]
        u = proj_ref[0, rows, COL_POOL:COL_POOL + w]
        x = jnp.concatenate([jnp.where(i > 0, halo, 0.0), u], axis=0)
        s2 = x + pltpu.roll(x, 1, axis=0)
        s4 = s2 + pltpu.roll(s2, 2, axis=0)
        s4_hi = s4[:, LANES:]
        s8 = s4_hi + pltpu.roll(s4_hi, 4, axis=0)
        s16 = s8 + pltpu.roll(s8, 8, axis=0)
        tpos = (pool_row + r0 + 1).astype(jnp.float32)
        halves = []
        for narrow, wide, win in ((s2[:, :LANES], s4[:, :LANES], POOL_WINDOWS[0]),
                                  (s8, s16, POOL_WINDOWS[2])):
            halves.append(jnp.where(pool_first_group,
                                    narrow[POOL_HALO:, :] / jnp.minimum(tpos, float(win)),
                                    wide[POOL_HALO:, :] / jnp.minimum(tpos, float(2 * win))))
        pooled = jnp.concatenate(halves, axis=1)
        mixed = jnp.dot((pooled - u).astype(jnp.bfloat16), pw_ref[...],
                        preferred_element_type=jnp.float32)
        pool_gate = _silu(proj_ref[0, rows, COL_POOL + w:COL_POOL + 2 * w])
        o_ref[0, rows, w:2 * w] = ((mixed * ps_ref[...]) * pool_gate).astype(o_ref.dtype)

    slot_of = (slots[0:5], slots[5:10])
    front(0, slot_of[0])

    def several_blocks(j, carry):
        i0 = HGRN_UNROLL * j
        for u in range(HGRN_UNROLL):
            front(jnp.minimum(i0 + u + 1, n_blocks - 1), slot_of[(u + 1) % 2])
            back(i0 + u, slot_of[u % 2])
        return carry

    lax.fori_loop(0, n_blocks // HGRN_UNROLL, several_blocks, 0)


def _hgrn_pool(pa, lower_bounds, gn, pool_w_blockdiag, pool_scale, layer):
    b, t, width = pa.shape
    out_width = HGRN_WIDTH + POOL_WIDTH
    return pl.pallas_call(
        functools.partial(_hgrn_pool_kernel, layer=layer),
        grid=(b,),
        in_specs=[
            pl.BlockSpec((DEPTH, HGRN_WIDTH), lambda i: (0, 0)),
            pl.BlockSpec((1, HGRN_WIDTH), lambda i: (0, 0)),
            pl.BlockSpec((POOL_WIDTH, POOL_WIDTH), lambda i: (0, 0)),
            pl.BlockSpec((1, POOL_WIDTH), lambda i: (0, 0)),
            pl.BlockSpec((1, t, width), lambda i: (i, 0, 0)),
        ],
        out_specs=pl.BlockSpec((1, t, out_width), lambda i: (i, 0, 0)),
        out_shape=jax.ShapeDtypeStruct((b, t, out_width), jnp.bfloat16),
        scratch_shapes=[pltpu.VMEM((HGRN_WIDTH // LANES, HEAD_DIM, LANES), jnp.float32)] + 2 * [
            pltpu.VMEM((HGRN_BLOCK * 3 * HGRN_CHUNK // 4, HGRN_WIDTH), jnp.float32),
            pltpu.VMEM((HGRN_BLOCK, HGRN_WIDTH), jnp.bfloat16),
            pltpu.VMEM((HGRN_WIDTH // LANES, HGRN_BLOCK, HGRN_BLOCK // HGRN_CHUNK * LANES),
                       jnp.bfloat16),
            pltpu.VMEM((max(SUBLANES, HGRN_BLOCK // HGRN_CHUNK), HGRN_WIDTH), jnp.float32),
            pltpu.VMEM((3, HGRN_WIDTH // LANES, HGRN_BLOCK, LANES), jnp.float32),
        ],
        compiler_params=pltpu.CompilerParams(
            dimension_semantics=("arbitrary",), vmem_limit_bytes=VMEM_LIMIT),
        name="hgrn2_pool",
    )(lower_bounds, gn, pool_w_blockdiag, pool_scale, pa)


def _fox_kernel(q_ref, k_ref, v_ref, g_ref, qx_ref, kx_ref, o_ref, kcat_ref, qcat_ref, vaug_ref,
                acc_ref, st_ref):
    first_pair = pl.program_id(1) * ATT_PAIRS
    t_len = k_ref.shape[1]
    bq = ATT_BLOCK
    tile = ATT_DIAG_TILE
    n_qblocks = t_len // bq

    lane = lax.broadcasted_iota(jnp.int32, (t_len, LANES), 1)
    qx = qx_ref[0]
    for pp in range(ATT_PAIRS):
        lanes = slice(pp * LANES, (pp + 1) * LANES)
        kcat_ref[pp, :, 0:LANES] = k_ref[0, :, lanes]
        kcat_ref[pp, :, LANES:2 * LANES] = kx_ref[0]
        vt = v_ref[0, :, lanes].astype(jnp.float32).T.astype(jnp.bfloat16)
        qf = q_ref[0, :, lanes].astype(jnp.float32) * (HEAD_DIM ** -0.5 * LOG2E)
        for h2 in range(2):
            vaug_ref[pp, h2, 0:HEAD_DIM, :] = vt[h2 * HEAD_DIM:(h2 + 1) * HEAD_DIM, :]
            vaug_ref[pp, h2, HEAD_DIM:, :] = jnp.ones((ATT_ONES_ROWS, t_len), jnp.bfloat16)
            qcat_ref[pp, h2, :, 0:LANES] = jnp.where(
                (lane // HEAD_DIM) == h2, qf, 0.0).astype(jnp.bfloat16)
            qcat_ref[pp, h2, :, LANES:2 * LANES] = jnp.where(
                (lane // BIAS_SLOT) == (first_pair + pp) * 2 + h2, qx, jnp.zeros_like(qx))
    acc_ref[...] = jnp.zeros_like(acc_ref)

    def pair_units(pp):
        out = []
        for qb in range(n_qblocks):
            for k0 in range(0, qb * bq, ATT_KEY_TILE):
                out += [(pp, qb, h2, qb * bq, bq, k0, min(ATT_KEY_TILE, qb * bq - k0), False)
                        for h2 in range(2)]
            for h2 in range(2):
                out += [(pp, qb, h2, qb * bq + c * tile, tile, qb * bq, (c + 1) * tile, True)
                        for c in range(bq // tile)]
        return out

    units = [u for group in zip(*[pair_units(pp) for pp in range(ATT_PAIRS)]) for u in group]

    def scores(slot, unit):
        pp, _, h2, q0, nq, k0, nk, _ = unit
        st_ref[slot, 0:nk, 0:nq] = lax.dot_general(
            kcat_ref[pp, k0:k0 + nk, :], qcat_ref[pp, h2, q0:q0 + nq, :], (((1,), (1,)), ((), ())),
            preferred_element_type=jnp.float32)

    m_run = {(pp, qb, h2, c): jnp.full((1, tile), MASK_VALUE, jnp.float32)
             for pp in range(ATT_PAIRS) for qb in range(n_qblocks) for h2 in range(2)
             for c in range(bq // tile)}

    def consume(slot, unit):
        pp, qb, h2, q0, nq, k0, nk, masked = unit
        if masked:
            key = lax.broadcasted_iota(jnp.int32, (nk, nq), 0) + k0
            query = lax.broadcasted_iota(jnp.int32, (nk, nq), 1) + q0
            st_ref[slot, 0:nk, 0:nq] = jnp.where(query >= key, st_ref[slot, 0:nk, 0:nq], MASK_VALUE)
        tiles = [(q0 - qb * bq) // tile + i for i in range(nq // tile)]
        m_prev = jnp.concatenate([m_run[(pp, qb, h2, c)] for c in tiles], axis=1)
        m_new = jnp.maximum(m_prev, jnp.max(st_ref[slot, 0:nk, 0:nq], axis=0, keepdims=True))
        pt = jnp.exp2(st_ref[slot, 0:nk, 0:nq] - m_new).astype(jnp.bfloat16)
        alpha = jnp.exp2(m_prev - m_new)
        cols = slice(q0 - qb * bq, q0 - qb * bq + nq)
        acc_ref[pp, qb, h2, :, cols] = acc_ref[pp, qb, h2, :, cols] * alpha + jnp.dot(
            vaug_ref[pp, h2, :, k0:k0 + nk], pt, preferred_element_type=jnp.float32)
        for i, c in enumerate(tiles):
            m_run[(pp, qb, h2, c)] = m_new[:, i * tile:(i + 1) * tile]

    def finish(pp, qb):
        ot = jnp.concatenate(
            [acc_ref[pp, qb, h2, 0:HEAD_DIM, :] / acc_ref[pp, qb, h2, HEAD_DIM:HEAD_DIM + 1, :]
             for h2 in range(2)], axis=0)
        rows = slice(qb * bq, (qb + 1) * bq)
        lanes = slice(pp * LANES, (pp + 1) * LANES)
        gate = _silu(g_ref[0, rows, lanes].astype(jnp.float32))
        o_ref[0, rows, lanes] = (ot.T * gate).astype(o_ref.dtype)

    ahead = ATT_SLOTS - 1
    for u in range(min(ahead, len(units))):
        scores(u % ATT_SLOTS, units[u])
    last_of_block = {}
    for u, unit in enumerate(units):
        last_of_block[unit[0:2]] = u
    for u, unit in enumerate(units):
        if u + ahead < len(units):
            scores((u + ahead) % ATT_SLOTS, units[u + ahead])
        consume(u % ATT_SLOTS, unit)
        if last_of_block[unit[0:2]] == u:
            finish(*unit[0:2])


def _fox(pc, qx, kx):
    b, t, _ = pc.shape
    width = ATT_PAIRS * LANES
    steps = FOX_WIDTH // width
    section = lambda sec: pl.BlockSpec((1, t, width), lambda i, p: (i, 0, sec * steps + p))
    shared = pl.BlockSpec((1, t, LANES), lambda i, p: (i, 0, 0))
    v_rows = HEAD_DIM + ATT_ONES_ROWS
    return pl.pallas_call(
        _fox_kernel,
        grid=(b, steps),
        in_specs=[section(0), section(1), section(2), section(3), shared, shared],
        out_specs=pl.BlockSpec((1, t, width), lambda i, p: (i, 0, p)),
        out_shape=jax.ShapeDtypeStruct((b, t, FOX_WIDTH), jnp.bfloat16),
        scratch_shapes=[
            pltpu.VMEM((ATT_PAIRS, t, 2 * LANES), jnp.bfloat16),
            pltpu.VMEM((ATT_PAIRS, 2, t, 2 * LANES), jnp.bfloat16),
            pltpu.VMEM((ATT_PAIRS, 2, v_rows, t), jnp.bfloat16),
            pltpu.VMEM((ATT_PAIRS, t // ATT_BLOCK, 2, v_rows, ATT_BLOCK), jnp.float32),
            pltpu.VMEM((ATT_SLOTS, max(ATT_KEY_TILE, ATT_BLOCK), ATT_BLOCK), jnp.float32),
        ],
        compiler_params=pltpu.CompilerParams(
            dimension_semantics=("arbitrary", "arbitrary"), vmem_limit_bytes=VMEM_LIMIT),
        name="fox_attention",
    )(pc, pc, pc, pc, qx, kx)


def _project_out(x_ref, ab_ref, c_ref, w_ref, g_ref):
    mixed = jnp.concatenate([ab_ref[...], c_ref[...]], axis=-1)
    y = jnp.dot(mixed, w_ref[...], preferred_element_type=jnp.float32)
    ms = jnp.mean(y * y, axis=-1, keepdims=True)
    return x_ref[...] + y * lax.rsqrt(ms + NORM_EPS) * g_ref[...]


def _outproj_kernel(x_ref, ab_ref, c_ref, w_ref, g_ref, o_ref):
    o_ref[...] = _project_out(x_ref, ab_ref, c_ref, w_ref, g_ref)


def _outproj(x2d, o_ab, o_c, w, g):
    m = x2d.shape[0]
    return pl.pallas_call(
        _outproj_kernel,
        grid=(m // OUT_ROWS,),
        in_specs=[
            _row_spec(D_MODEL, OUT_ROWS), _row_spec(HGRN_WIDTH + POOL_WIDTH, OUT_ROWS),
            _row_spec(FOX_WIDTH, OUT_ROWS),
            _whole_spec((D_MODEL, D_MODEL)), _whole_spec((1, D_MODEL)),
        ],
        out_specs=_row_spec(D_MODEL, OUT_ROWS),
        out_shape=jax.ShapeDtypeStruct((m, D_MODEL), jnp.float32),
        compiler_params=pltpu.CompilerParams(
            dimension_semantics=("arbitrary",), vmem_limit_bytes=VMEM_LIMIT),
        name="outproj",
    )(x2d, o_ab, o_c, w, g)


def kernel(x, lower_bounds, pre_norm_g, w_in, hgrn_norm_g, fox_f_bias, pool_w, pool_scale, w_out,
           post_norm_g):
    b, t, d = x.shape
    m = b * t
    x2d = x.reshape(m, d)
    w_in_t = jnp.swapaxes(w_in, 1, 2)
    for layer in range(DEPTH):
        bias_pad = jnp.pad(fox_f_bias[layer], (0, LANES - FOX_HEADS)).reshape(1, LANES)
        pool_bd = jax.scipy.linalg.block_diag(*pool_w[layer]).astype(jnp.bfloat16)

        pa, pc, pf = _inproj(x2d, pre_norm_g[layer].reshape(1, d), w_in_t, layer)
        qx, kx = _forget_cumsum(pf.reshape(b, t, LANES), bias_pad)
        o_ab = _hgrn_pool(pa.reshape(b, t, COL_Q), lower_bounds,
                          hgrn_norm_g[layer].reshape(1, HGRN_WIDTH), pool_bd,
                          pool_scale[layer].reshape(1, POOL_WIDTH), layer)
        o_c = _fox(pc.reshape(b, t, COL_FC - COL_Q), qx, kx)
        x2d = _outproj(x2d, o_ab.reshape(m, HGRN_WIDTH + POOL_WIDTH), o_c.reshape(m, FOX_WIDTH),
                       w_out[layer].astype(jnp.bfloat16), post_norm_g[layer].reshape(1, d))
    return x2d.reshape(b, t, d)
```
